```python
import numpy as np
import jax
import jax.numpy as jnp
from jax import lax

D_MODEL = 1024
BATCH = 16
SEQ = 2048
DEPTH = 2
DEC_BATCH = 32
DEC_SEQ = 8
PAST_LEN = 16384
PAGE_SIZE = 128

N_HEADS = 16
HEAD_DIM = D_MODEL // N_HEADS
N_KV = 4
GROUP = N_HEADS // N_KV
CMP_LEN = 32
CMP_STRIDE = 16
CMP_PARTS = CMP_LEN // CMP_STRIDE
CMP_HIDDEN = 2 * HEAD_DIM
SEL_BLOCK = 64
N_SEL = 16
WINDOW = 512
WIN_QBLK = 128
SEL_QCHUNK = 16
C_CONV = D_MODEL
CONV_W = 31
D_FF = -(-(8 * D_MODEL) // (3 * 256)) * 256
PLE_DIM = 256
ALPHA = (2 * DEPTH) ** 0.25
BETA = (8 * DEPTH) ** -0.25
LN_EPS = 1e-5
ATTN_SCALE = HEAD_DIM ** -0.5
FORCE_SCORE = 1e4
IN_SIZES = (N_HEADS * HEAD_DIM, 3 * N_KV * 2 * HEAD_DIM, 3 * N_HEADS, 2 * C_CONV, 2 * D_MODEL)
N_IN = sum(IN_SIZES)

kernel_name = 'nsa_conformer_hybrid_step'


def _alibi_slopes():
    h = jnp.arange(1, N_HEADS + 1, dtype=jnp.float32)
    return jnp.exp2(-8.0 * h / N_HEADS).reshape(N_KV, GROUP)


def _layernorm(x, g, b):
    xf = x.astype(jnp.float32)
    mu = xf.mean(-1, keepdims=True)
    var = jnp.square(xf - mu).mean(-1, keepdims=True)
    return ((xf - mu) * lax.rsqrt(var + LN_EPS) * g + b).astype(x.dtype)


def _masked_softmax(s, mask):
    s = jnp.where(mask, s, -jnp.inf)
    m = jnp.max(s, axis=-1, keepdims=True)
    m = jnp.where(jnp.isfinite(m), m, 0.0)
    e = jnp.where(mask, jnp.exp(s - m), 0.0)
    return e / jnp.maximum(e.sum(-1, keepdims=True), 1e-30)


def _compress(kv, w1, pe, b1, w2, b2):
    B, L = kv.shape[:2]
    n_ch = L // CMP_STRIDE
    nb = n_ch - CMP_PARTS + 1
    ch = kv[:, :n_ch * CMP_STRIDE].reshape(B, n_ch, CMP_STRIDE, N_KV, 2, HEAD_DIM)
    w1p = w1.reshape(2, CMP_PARTS, CMP_STRIDE, HEAD_DIM, CMP_HIDDEN)
    h = b1 + jnp.einsum('sjd,sjdh->sh', pe, w1)
    for r in range(CMP_PARTS):
        h = h + jnp.einsum('bcjgsd,sjdh->bcgsh', ch, w1p[:, r])[:, r:r + nb]
    return jnp.einsum('bcgsh,shd->bcgsd', jax.nn.silu(h), w2) + b2


def _cmp_attend(q, kc, q_pos, slopes):
    nb = kc.shape[1]
    blk_end = jnp.arange(nb) * CMP_STRIDE + (CMP_LEN - 1)
    dist = q_pos[:, None] - blk_end[None, :]
    s = jnp.einsum('btgrd,bngd->btgrn', q, kc[..., 0, :], preferred_element_type=jnp.float32) * ATTN_SCALE
    s = s - slopes[:, :, None] * dist.astype(jnp.float32)[:, None, None, :]
    p = _masked_softmax(s, (dist >= 0)[:, None, None, :])
    o = jnp.einsum('btgrn,bngd->btgrd', p.astype(kc.dtype), kc[..., 1, :])
    return o, p


def _select_blocks(p, q_pos, n_keys):
    nb = p.shape[-1]
    nsb = -(-n_keys // SEL_BLOCK)
    cs = jnp.arange(nb) * CMP_STRIDE
    ss = jnp.arange(nsb) * SEL_BLOCK
    overlap = ((cs[:, None] < ss[None, :] + SEL_BLOCK) & (cs[:, None] + CMP_LEN > ss[None, :])).astype(jnp.float32)
    imp = jnp.einsum('btgn,nj->btgj', p.sum(axis=3), overlap)
    cur = (q_pos // SEL_BLOCK)[:, None, None]
    j = jnp.arange(nsb)
    forced = (j == 0) | (j == cur) | (j == cur - 1)
    score = jnp.where(j > cur, -1.0, jnp.where(forced, FORCE_SCORE, imp))
    top, idx = lax.top_k(score, min(N_SEL, nsb))
    return idx, top > -0.5


def _sel_attend(q, kvg, key_pos, key_ok, q_pos, slopes):
    s = jnp.einsum('btgrd,btgnkd->btgrnk', q, kvg[..., 0, :], preferred_element_type=jnp.float32) * ATTN_SCALE
    dist = q_pos[None, :, None, None, None] - key_pos
    s = s - slopes[None, None, :, :, None, None] * dist[:, :, :, None].astype(jnp.float32)
    mask = (key_ok & (dist >= 0))[:, :, :, None]
    B, T, G, R, N, K = s.shape
    p = _masked_softmax(s.reshape(B, T, G, R, N * K), mask.reshape(B, T, G, 1, N * K)).reshape(s.shape)
    return jnp.einsum('btgrnk,btgnkd->btgrd', p.astype(kvg.dtype), kvg[..., 1, :])


def _win_attend(q, kv, q_pos, k_pos, slopes):
    s = jnp.einsum('btgrd,bsgd->btgrs', q, kv[..., 0, :], preferred_element_type=jnp.float32) * ATTN_SCALE
    dist = q_pos[:, None] - k_pos[None, :]
    mask = (dist >= 0) & (dist < WINDOW) & (k_pos[None, :] >= 0)
    s = s - slopes[:, :, None] * dist.astype(jnp.float32)[:, None, None, :]
    p = _masked_softmax(s, mask[:, None, None, :])
    return jnp.einsum('btgrs,bsgd->btgrd', p.astype(kv.dtype), kv[..., 1, :])


def _nsa_prompt(q, kvc, kvs, kvw, cw, slopes):
    B, S = q.shape[:2]
    pos = jnp.arange(S)
    kc = _compress(kvc, *cw)
    o_c, p = _cmp_attend(q, kc, pos, slopes)
    idx, ok = _select_blocks(p, pos, S)
    kvs_g = kvs.transpose(0, 2, 1, 3, 4)
    bi = jnp.arange(B)[:, None, None, None, None]
    gi = jnp.arange(N_KV)[None, None, :, None, None]

    def sel_chunk(t0):
        qc = lax.dynamic_slice_in_dim(q, t0, SEL_QCHUNK, axis=1)
        ic = lax.dynamic_slice_in_dim(idx, t0, SEL_QCHUNK, axis=1)
        okc = lax.dynamic_slice_in_dim(ok, t0, SEL_QCHUNK, axis=1)
        kpos = ic[..., None] * SEL_BLOCK + jnp.arange(SEL_BLOCK)
        kvg = kvs_g[bi, gi, jnp.minimum(kpos, S - 1)]
        return _sel_attend(qc, kvg, kpos, okc[..., None], t0 + jnp.arange(SEL_QCHUNK), slopes)

    o_s = lax.map(sel_chunk, jnp.arange(0, S, SEL_QCHUNK))
    o_s = jnp.moveaxis(o_s, 0, 1).reshape(q.shape)
    kw_pad = jnp.pad(kvw, ((0, 0), (WINDOW, 0), (0, 0), (0, 0), (0, 0)))

    def win_block(t0):
        qc = lax.dynamic_slice_in_dim(q, t0, WIN_QBLK, axis=1)
        kb = lax.dynamic_slice_in_dim(kw_pad, t0, WIN_QBLK + WINDOW, axis=1)
        return _win_attend(qc, kb, t0 + jnp.arange(WIN_QBLK), t0 - WINDOW + jnp.arange(WIN_QBLK + WINDOW), slopes)

    o_w = lax.map(win_block, jnp.arange(0, S, WIN_QBLK))
    o_w = jnp.moveaxis(o_w, 0, 1).reshape(q.shape)
    return o_c, o_s, o_w, (kvc, kvs, kvw[:, S - min(WINDOW, S):])


def _nsa_sample(q, kvc, kvs, kvw, cw, slopes, cache_cmp, cache_sel, win_buf, page_table, layer):
    B, T = q.shape[:2]
    past_len = page_table.shape[1] * PAGE_SIZE
    pos = past_len + jnp.arange(T)
    past_c = cache_cmp[layer, page_table].reshape(B, past_len, N_KV, 2, HEAD_DIM)
    kc = _compress(jnp.concatenate([past_c, kvc], axis=1), *cw)
    o_c, p = _cmp_attend(q, kc, pos, slopes)
    idx, ok = _select_blocks(p, pos, past_len + T)
    kpos = idx[..., None] * SEL_BLOCK + jnp.arange(SEL_BLOCK)
    bi = jnp.arange(B)[:, None, None, None, None]
    gi = jnp.arange(N_KV)[None, None, :, None, None]
    pc = jnp.minimum(kpos, past_len - 1)
    past_g = cache_sel[layer, page_table[bi, pc // PAGE_SIZE], pc % PAGE_SIZE, gi]
    new_g = kvs[bi, jnp.clip(kpos - past_len, 0, T - 1), gi]
    kvg = jnp.where((kpos < past_len)[..., None, None], past_g, new_g)
    o_s = _sel_attend(q, kvg, kpos, ok[..., None], pos, slopes)
    wb = win_buf.shape[1]
    kw = jnp.concatenate([win_buf, kvw], axis=1)
    o_w = _win_attend(q, kw, pos, past_len - wb + jnp.arange(wb + T), slopes)
    return o_c, o_s, o_w, (kvc, kvs, kw[:, T:])


def _conv_module(u, buf, conv_w, conv_b, ln_g, ln_b, w_pw2, b_pw2):
    a, g = jnp.split(u, 2, axis=-1)
    z = a * jax.nn.sigmoid(g)
    if buf is None:
        buf = jnp.zeros((z.shape[0], CONV_W - 1, C_CONV), z.dtype)
    zc = jnp.concatenate([buf, z], axis=1)
    y = lax.conv_general_dilated(zc, conv_w[:, None, :], (1,), 'VALID',
                                 dimension_numbers=('NWC', 'WIO', 'NWC'),
                                 feature_group_count=C_CONV) + conv_b
    y = jax.nn.silu(_layernorm(y, ln_g, ln_b))
    return y @ w_pw2 + b_pw2, zc[:, -(CONV_W - 1):]


def _trunk(x, ple, nsa_mixer, conv_buf, w_in, b_in, cmp_w1, cmp_pe, cmp_b1, cmp_w2, cmp_b2,
           conv_w, conv_b, conv_ln_g, conv_ln_b, w_pw2, b_pw2, w_out, ln1_g, ln1_b,
           w_ffn_gate, w_ffn_up, w_ffn_down, ln2_g, ln2_b, w_ple_in, w_ple_gate):
    B, T = x.shape[:2]
    slopes = _alibi_slopes()
    offs = np.cumsum(IN_SIZES)[:-1].tolist()
    new_c, new_s, new_w, new_cv = [], [], [], []
    for l in range(DEPTH):
        h = x @ w_in[l] + b_in[l]
        q, kv, gb, u, gm = jnp.split(h, offs, axis=-1)
        q = q.reshape(B, T, N_KV, GROUP, HEAD_DIM)
        kv = kv.reshape(B, T, 3, N_KV, 2, HEAD_DIM)
        cw = (cmp_w1[l], cmp_pe[l], cmp_b1[l], cmp_w2[l], cmp_b2[l])
        o_c, o_s, o_w, (st_c, st_s, st_w) = nsa_mixer(l, q, kv[:, :, 0], kv[:, :, 1], kv[:, :, 2], cw, slopes)
        g3 = jax.nn.sigmoid(gb).reshape(B, T, 3, N_KV, GROUP, 1)
        o_a = (g3[:, :, 0] * o_c + g3[:, :, 1] * o_s + g3[:, :, 2] * o_w).reshape(B, T, N_HEADS * HEAD_DIM)
        o_b, st_cv = _conv_module(u, conv_buf(l), conv_w[l], conv_b[l], conv_ln_g[l], conv_ln_b[l], w_pw2[l], b_pw2[l])
        gm = jax.nn.sigmoid(gm).reshape(B, T, 2, D_MODEL)
        mixed = gm[:, :, 0] * o_a + gm[:, :, 1] * o_b
        x = _layernorm(ALPHA * x + mixed @ w_out[l], ln1_g[l], ln1_b[l])
        f = (jax.nn.silu(x @ w_ffn_gate[l]) * (x @ w_ffn_up[l])) @ w_ffn_down[l]
        x = _layernorm(ALPHA * x + f, ln2_g[l], ln2_b[l])
        x = x + jax.nn.sigmoid(x @ w_ple_gate[l]) * (ple[l] @ w_ple_in[l])
        new_c.append(st_c)
        new_s.append(st_s)
        new_w.append(st_w)
        new_cv.append(st_cv)
    return x, jnp.stack(new_c), jnp.stack(new_s), jnp.stack(new_w), jnp.stack(new_cv)


def setup_inputs(seed: int = 0) -> dict:
    key = jax.random.key(seed)
    ks = iter(jax.random.split(key, 40))

    def nrm(shape, scale):
        return scale * jax.random.normal(next(ks), shape, jnp.float32)

    n_pages = PAST_LEN // PAGE_SIZE
    n_used = DEC_BATCH * n_pages
    n_pool = n_used + n_used // 4
    win_buf = min(WINDOW, PAST_LEN)
    page_table = jax.random.permutation(next(ks), n_pool)[:n_used].reshape(DEC_BATCH, n_pages).astype(jnp.int32)
    return {
        'x_prompt': nrm((BATCH, SEQ, D_MODEL), 1.0),
        'x_sample': nrm((DEC_BATCH, DEC_SEQ, D_MODEL), 1.0),
        'cache_cmp': nrm((DEPTH, n_pool, PAGE_SIZE, N_KV, 2, HEAD_DIM), 1.0),
        'cache_sel': nrm((DEPTH, n_pool, PAGE_SIZE, N_KV, 2, HEAD_DIM), 1.0),
        'state_win': nrm((DEPTH, DEC_BATCH, win_buf, N_KV, 2, HEAD_DIM), 1.0),
        'state_conv': nrm((DEPTH, DEC_BATCH, CONV_W - 1, C_CONV), 0.5),
        'page_table': page_table,
        'p_prompt': nrm((DEPTH, BATCH, SEQ, PLE_DIM), 1.0),
        'p_sample': nrm((DEPTH, DEC_BATCH, DEC_SEQ, PLE_DIM), 1.0),
        'w_in': nrm((DEPTH, D_MODEL, N_IN), D_MODEL ** -0.5),
        'b_in': nrm((DEPTH, N_IN), 0.02),
        'cmp_w1': nrm((DEPTH, 2, CMP_LEN, HEAD_DIM, CMP_HIDDEN), (CMP_LEN * HEAD_DIM) ** -0.5),
        'cmp_pe': nrm((DEPTH, 2, CMP_LEN, HEAD_DIM), 0.1),
        'cmp_b1': nrm((DEPTH, 2, CMP_HIDDEN), 0.02),
        'cmp_w2': nrm((DEPTH, 2, CMP_HIDDEN, HEAD_DIM), CMP_HIDDEN ** -0.5),
        'cmp_b2': nrm((DEPTH, 2, HEAD_DIM), 0.02),
        'conv_w': nrm((DEPTH, CONV_W, C_CONV), CONV_W ** -0.5),
        'conv_b': nrm((DEPTH, C_CONV), 0.02),
        'conv_ln_g': 1.0 + nrm((DEPTH, C_CONV), 0.02),
        'conv_ln_b': nrm((DEPTH, C_CONV), 0.02),
        'w_pw2': nrm((DEPTH, C_CONV, D_MODEL), C_CONV ** -0.5),
        'b_pw2': nrm((DEPTH, D_MODEL), 0.02),
        'w_out': nrm((DEPTH, D_MODEL, D_MODEL), BETA * D_MODEL ** -0.5),
        'ln1_g': 1.0 + nrm((DEPTH, D_MODEL), 0.02),
        'ln1_b': nrm((DEPTH, D_MODEL), 0.02),
        'w_ffn_gate': nrm((DEPTH, D_MODEL, D_FF), D_MODEL ** -0.5),
        'w_ffn_up': nrm((DEPTH, D_MODEL, D_FF), D_MODEL ** -0.5),
        'w_ffn_down': nrm((DEPTH, D_FF, D_MODEL), BETA * D_FF ** -0.5),
        'ln2_g': 1.0 + nrm((DEPTH, D_MODEL), 0.02),
        'ln2_b': nrm((DEPTH, D_MODEL), 0.02),
        'w_ple_in': nrm((DEPTH, PLE_DIM, D_MODEL), PLE_DIM ** -0.5),
        'w_ple_gate': nrm((DEPTH, D_MODEL, D_MODEL), D_MODEL ** -0.5),
    }


def reference(x_prompt, x_sample, cache_cmp, cache_sel, state_win, state_conv, page_table, p_prompt, p_sample,
              w_in, b_in, cmp_w1, cmp_pe, cmp_b1, cmp_w2, cmp_b2, conv_w, conv_b, conv_ln_g, conv_ln_b,
              w_pw2, b_pw2, w_out, ln1_g, ln1_b, w_ffn_gate, w_ffn_up, w_ffn_down, ln2_g, ln2_b,
              w_ple_in, w_ple_gate):
    weights = (w_in, b_in, cmp_w1, cmp_pe, cmp_b1, cmp_w2, cmp_b2, conv_w, conv_b, conv_ln_g, conv_ln_b,
               w_pw2, b_pw2, w_out, ln1_g, ln1_b, w_ffn_gate, w_ffn_up, w_ffn_down, ln2_g, ln2_b,
               w_ple_in, w_ple_gate)

    def prompt_mixer(l, q, kvc, kvs, kvw, cw, slopes):
        return _nsa_prompt(q, kvc, kvs, kvw, cw, slopes)

    def sample_mixer(l, q, kvc, kvs, kvw, cw, slopes):
        return _nsa_sample(q, kvc, kvs, kvw, cw, slopes, cache_cmp, cache_sel, state_win[l], page_table, l)

    y_prompt, new_cmp_prompt, new_sel_prompt, new_win_prompt, new_conv_prompt = _trunk(
        x_prompt, p_prompt, prompt_mixer, lambda l: None, *weights)
    y_sample, new_cmp_sample, new_sel_sample, new_win_sample, new_conv_sample = _trunk(
        x_sample, p_sample, sample_mixer, lambda l: state_conv[l], *weights)
    return (y_prompt, y_sample, new_cmp_prompt, new_sel_prompt, new_win_prompt, new_conv_prompt,
            new_cmp_sample, new_sel_sample, new_win_sample, new_conv_sample)
```

```python
import functools

import numpy as np
import jax
import jax.numpy as jnp
from jax import lax
from jax.experimental import pallas as pl
from jax.experimental.pallas import tpu as pltpu

F32 = jnp.float32
BF16 = jnp.bfloat16

D_MODEL = 1024
N_HEADS = 16
HEAD_DIM = 64
N_KV = 4
GROUP = N_HEADS // N_KV
KV_ROW = N_KV * 2 * HEAD_DIM
CMP_LEN = 32
CMP_STRIDE = 16
CMP_PARTS = CMP_LEN // CMP_STRIDE
CMP_HIDDEN = 2 * HEAD_DIM
SEL_BLOCK = 64
SEL_SHIFT = SEL_BLOCK.bit_length() - 1
N_SEL = 16
WINDOW = 512
C_CONV = D_MODEL
CONV_W = 31
PAGE_SIZE = 128
LN_EPS = 1e-5
ATTN_SCALE = HEAD_DIM ** -0.5
FORCE_SCORE = 1e4
NEG = -1e30
LANES = 128
VMEM_LIMIT = 56 * 1024 * 1024

PAGES_PER_STEP = 16
TQ = 256
KC = 256


def _dot(a, b):
    return jnp.dot(a, b, preferred_element_type=F32)


def _dot_nt(a, b):
    return lax.dot_general(a, b, (((1,), (1,)), ((), ())), preferred_element_type=F32)


def _dot_split3(a, b_bf16):
    a1 = a.astype(BF16)
    r1 = a - a1.astype(F32)
    a2 = r1.astype(BF16)
    a3 = (r1 - a2.astype(F32)).astype(BF16)
    return _dot(a1, b_bf16) + _dot(a2, b_bf16) + _dot(a3, b_bf16)


def _sigmoid(x):
    return 1.0 / (1.0 + jnp.exp(-x))


def _layernorm(x, g, b):
    mu = jnp.mean(x, axis=-1, keepdims=True)
    xc = x - mu
    var = jnp.mean(xc * xc, axis=-1, keepdims=True)
    return xc * lax.rsqrt(var + LN_EPS) * g + b


def _params(*sem):
    return pltpu.CompilerParams(dimension_semantics=sem, vmem_limit_bytes=VMEM_LIMIT)


def _full(shape):
    n = len(shape)
    return pl.BlockSpec(shape, lambda *_: (0,) * n)


def _resident(shape):
    n = len(shape)
    return pl.BlockSpec(shape, lambda *_: (0,) * n, pipeline_mode=pl.Buffered(1))


def _pe_term_kernel(pe_ref, w1_ref, b1_ref, o_ref):
    o_ref[...] = b1_ref[...] + jnp.sum(pe_ref[...] * w1_ref[...], axis=0, keepdims=True)


def _pe_term(cmp_pe, cmp_w1, cmp_b1):
    n = cmp_pe.shape[0] * 2
    k = CMP_LEN * HEAD_DIM
    out = pl.pallas_call(
        _pe_term_kernel,
        grid=(n,),
        in_specs=[pl.BlockSpec((None, k, 1), lambda i: (i, 0, 0)),
                  pl.BlockSpec((None, k, CMP_HIDDEN), lambda i: (i, 0, 0)),
                  pl.BlockSpec((None, 1, CMP_HIDDEN), lambda i: (i, 0, 0))],
        out_specs=pl.BlockSpec((None, 1, CMP_HIDDEN), lambda i: (i, 0, 0)),
        out_shape=jax.ShapeDtypeStruct((n, 1, CMP_HIDDEN), F32),
        compiler_params=_params("parallel"),
        name="pe_term",
    )(cmp_pe.reshape(n, k, 1), cmp_w1.reshape(n, k, CMP_HIDDEN), cmp_b1.reshape(n, 1, CMP_HIDDEN))
    return out.reshape(-1, 2, CMP_HIDDEN)


def _in_proj_kernel(x_ref, wq_ref, wkv_ref, wgb_ref, wu_ref, wgm_ref,
                    bq_ref, bkv_ref, bgb_ref, bu_ref, bgm_ref,
                    q_ref, kvc_ref, kvs_ref, kvw_ref, gb_ref, z_ref, gm_ref):
    xb = x_ref[...].astype(BF16)
    q_ref[...] = (_dot(xb, wq_ref[...]) + bq_ref[...]).astype(q_ref.dtype)
    kv = _dot(xb, wkv_ref[...]) + bkv_ref[...]
    kvc_ref[...] = kv[:, 0:KV_ROW]
    kvs_ref[...] = kv[:, KV_ROW:2 * KV_ROW]
    kvw_ref[...] = kv[:, 2 * KV_ROW:3 * KV_ROW]
    gb_ref[...] = _sigmoid(_dot(xb, wgb_ref[...]) + bgb_ref[...])
    u = _dot(xb, wu_ref[...]) + bu_ref[...]
    z_ref[...] = u[:, 0:C_CONV] * _sigmoid(u[:, C_CONV:2 * C_CONV])
    gm_ref[...] = _sigmoid(_dot(xb, wgm_ref[...]) + bgm_ref[...])


def _in_proj(x, w, q_dtype, tm):
    t = x.shape[0]
    row = lambda n: pl.BlockSpec((tm, n), lambda i: (i, 0))
    widths = (D_MODEL, 3 * KV_ROW, N_KV * LANES, 2 * C_CONV, 2 * D_MODEL)
    return pl.pallas_call(
        _in_proj_kernel,
        grid=(t // tm,),
        in_specs=[row(D_MODEL)] + [_resident((D_MODEL, n)) for n in widths] + [_resident((1, n)) for n in widths],
        out_specs=[row(D_MODEL), row(KV_ROW), row(KV_ROW), row(KV_ROW), row(N_KV * LANES), row(C_CONV),
                   row(2 * D_MODEL)],
        out_shape=[jax.ShapeDtypeStruct((t, D_MODEL), q_dtype),
                   jax.ShapeDtypeStruct((t, KV_ROW), F32),
                   jax.ShapeDtypeStruct((t, KV_ROW), F32),
                   jax.ShapeDtypeStruct((t, KV_ROW), F32),
                   jax.ShapeDtypeStruct((t, N_KV * LANES), F32),
                   jax.ShapeDtypeStruct((t, C_CONV), F32),
                   jax.ShapeDtypeStruct((t, 2 * D_MODEL), F32)],
        compiler_params=_params("parallel"),
        name="in_proj",
    )(x, w["wq"], w["wkv"], w["wgb"], w["wu"], w["wgm"], w["bq"], w["bkv"], w["bgb"], w["bu"], w["bgm"])


def _cmp_partial_kernel(*refs, n_src):
    if n_src > 1:
        refs = refs[1:]
    x_refs, w_ref, p_ref = refs[:n_src], refs[n_src], refs[n_src + 1]
    for g in range(N_KV):
        acc = None
        for j in range(CMP_STRIDE):
            lo = j * KV_ROW + g * 2 * HEAD_DIM
            parts = [xr[:, lo:lo + 2 * HEAD_DIM] for xr in x_refs]
            a = parts[0] if n_src == 1 else jnp.concatenate(parts, axis=0)
            d = _dot(a.astype(BF16), w_ref[j])
            acc = d if acc is None else acc + d
        p_ref[:, g * 4 * CMP_HIDDEN:(g + 1) * 4 * CMP_HIDDEN] = acc


def _cmp_partial_prompt(kvc, w1cat, batch):
    seq = kvc.shape[0] // batch
    n_ch = seq // CMP_STRIDE
    x = kvc.reshape(batch, n_ch, CMP_STRIDE * KV_ROW)
    return pl.pallas_call(
        functools.partial(_cmp_partial_kernel, n_src=1),
        grid=(batch,),
        in_specs=[pl.BlockSpec((None, n_ch, CMP_STRIDE * KV_ROW), lambda b: (b, 0, 0)),
                  _resident(w1cat.shape)],
        out_specs=pl.BlockSpec((None, n_ch, N_KV * 4 * CMP_HIDDEN), lambda b: (b, 0, 0)),
        out_shape=jax.ShapeDtypeStruct((batch, n_ch, N_KV * 4 * CMP_HIDDEN), F32),
        compiler_params=_params("parallel"),
        name="cmp_partial_prompt",
    )(x, w1cat)


def _cmp_partial_sample(cache_cmp, page_table, w1cat, layer):
    batch, n_pages = page_table.shape
    n_pool = cache_cmp.shape[1]
    ch_page = PAGE_SIZE // CMP_STRIDE
    x = cache_cmp.reshape(cache_cmp.shape[0], n_pool, ch_page, CMP_STRIDE * KV_ROW)
    n_step = n_pages // PAGES_PER_STEP
    rows = PAGES_PER_STEP * ch_page

    def page_spec(k):
        return pl.BlockSpec((None, None, ch_page, CMP_STRIDE * KV_ROW),
                            lambda b, c, pt: (layer, pt[b, c * PAGES_PER_STEP + k], 0, 0))

    return pl.pallas_call(
        functools.partial(_cmp_partial_kernel, n_src=PAGES_PER_STEP),
        grid_spec=pltpu.PrefetchScalarGridSpec(
            num_scalar_prefetch=1,
            grid=(batch, n_step),
            in_specs=[page_spec(k) for k in range(PAGES_PER_STEP)]
                     + [pl.BlockSpec(w1cat.shape, lambda b, c, pt: (0, 0, 0), pipeline_mode=pl.Buffered(1))],
            out_specs=pl.BlockSpec((None, rows, N_KV * 4 * CMP_HIDDEN), lambda b, c, pt: (b, c, 0)),
        ),
        out_shape=jax.ShapeDtypeStruct((batch, n_step * rows, N_KV * 4 * CMP_HIDDEN), F32),
        compiler_params=_params("parallel", "parallel"),
        name="cmp_partial_sample",
    )(page_table, *([x] * PAGES_PER_STEP), w1cat)


def _compress_finish(pb, hc_ref, w2_ref, b2_ref):
    n = pb.shape[0]
    out = []
    for s in range(2):
        p0 = pb[:, s * 2 * CMP_HIDDEN:s * 2 * CMP_HIDDEN + CMP_HIDDEN]
        p1 = pb[:, s * 2 * CMP_HIDDEN + CMP_HIDDEN:(s + 1) * 2 * CMP_HIDDEN]
        h = hc_ref[s:s + 1, :] + p0 + pltpu.roll(p1, n - 1, 0)
        a = h * _sigmoid(h)
        out.append((_dot(a.astype(BF16), w2_ref[s]) + b2_ref[s:s + 1, :]).astype(BF16))
    return out


def _softmax_rows(s, valid):
    s = jnp.where(valid, s, NEG)
    m = jnp.max(s, axis=-1, keepdims=True)
    e = jnp.where(valid, jnp.exp(s - m), 0.0)
    return e * (1.0 / jnp.maximum(jnp.sum(e, axis=-1, keepdims=True), 1e-30))


def _topk_mask(score, n_top):
    col = lax.broadcasted_iota(jnp.int32, score.shape, 1).astype(F32)
    big = float(score.shape[1])
    sel = jnp.zeros(score.shape, F32)
    for _ in range(n_top):
        m = jnp.max(score, axis=-1, keepdims=True)
        idx = jnp.min(jnp.where(score == m, col, big), axis=-1, keepdims=True)
        hit = col == idx
        sel = jnp.where(hit & (m > -0.5), 1.0, sel)
        score = jnp.where(hit, -3e38, score)
    return sel


def _block_scores(imp, q_pos):
    j = lax.broadcasted_iota(jnp.int32, imp.shape, 1)
    cur = jnp.right_shift(q_pos, SEL_SHIFT)
    forced = (j == 0) | (j == cur) | (j == cur - 1)
    return jnp.where(j > cur, -1.0, jnp.where(forced, FORCE_SCORE, imp))


def _online_update(s, valid, v, m, l, acc):
    s = jnp.where(valid, s, NEG)
    m_new = jnp.maximum(m, jnp.max(s, axis=-1, keepdims=True))
    alpha = jnp.exp(m - m_new)
    e = jnp.where(valid, jnp.exp(s - m_new), 0.0)
    l = alpha * l + jnp.sum(e, axis=-1, keepdims=True)
    acc = alpha * acc + _dot(e.astype(BF16), v)
    return m_new, l, acc


def _overlap_matrix(n_blk, n_blk_pad, n_sel_pad):
    n = np.arange(n_blk_pad)[:, None]
    j = np.arange(n_sel_pad)[None, :]
    ov = (n * CMP_STRIDE < j * SEL_BLOCK + SEL_BLOCK) & (n * CMP_STRIDE + CMP_LEN > j * SEL_BLOCK) & (n < n_blk)
    return jnp.asarray(ov, dtype=BF16)


def _prompt_attn_kernel(slopes_ref, q_ref, p_ref, kvs_ref, kvw_ref, gb_ref, hc_ref, w2_ref, b2_ref, ov_ref,
                        o_ref, kc_ref, vc_ref, *, n_blk):
    g = pl.program_id(1)
    i = pl.program_id(2)
    t0 = i * TQ

    @pl.when(i == 0)
    def _():
        kc, vc = _compress_finish(p_ref[...], hc_ref, w2_ref, b2_ref)
        kc_ref[...] = kc
        vc_ref[...] = vc

    q = q_ref[...]
    q_heads = [q[:, r * HEAD_DIM:(r + 1) * HEAD_DIM] for r in range(GROUP)]
    slopes = [slopes_ref[g * GROUP + r] for r in range(GROUP)]
    t_rel = lax.broadcasted_iota(jnp.int32, (TQ, 1), 0)

    n_pad = kc_ref.shape[0]
    n_idx = lax.broadcasted_iota(jnp.int32, (1, n_pad), 1)
    end_rel = n_idx * CMP_STRIDE + (CMP_LEN - 1) - t0
    valid_c = (t_rel >= end_rel) & (n_idx < n_blk)
    end_rel_f = end_rel.astype(F32)
    kc = kc_ref[...]
    vc = vc_ref[...]
    o_c = []
    p_sum = None
    for r in range(GROUP):
        p = _softmax_rows(_dot_nt(q_heads[r], kc) + slopes[r] * end_rel_f, valid_c)
        o_c.append(_dot(p.astype(BF16), vc))
        p_sum = p if p_sum is None else p_sum + p
    imp = _dot_split3(p_sum, ov_ref[...])
    sel = _topk_mask(_block_scores(imp, t_rel + t0), N_SEL).astype(BF16)

    def chunk_update(kv_ref, c, carry, valid_fn):
        kv = kv_ref[pl.ds(pl.multiple_of(c * KC, KC), KC), :]
        k = kv[:, 0:HEAD_DIM].astype(BF16)
        v = kv[:, HEAD_DIM:2 * HEAD_DIM].astype(BF16)
        k_rel = lax.broadcasted_iota(jnp.int32, (1, KC), 1) + (c * KC - t0)
        valid = valid_fn(c, k_rel)
        k_rel_f = k_rel.astype(F32)
        out = []
        for r in range(GROUP):
            m, l, acc = carry[r]
            out.append(_online_update(_dot_nt(q_heads[r], k) + slopes[r] * k_rel_f, valid, v, m, l, acc))
        return tuple(out)

    def sel_valid(c, k_rel):
        jj = lax.broadcasted_iota(jnp.int32, (LANES, KC), 0)
        kk = lax.broadcasted_iota(jnp.int32, (LANES, KC), 1)
        expand = jnp.where(jj == jnp.right_shift(kk + c * KC, SEL_SHIFT), 1.0, 0.0).astype(BF16)
        return (_dot(sel, expand) > 0.5) & (k_rel <= t_rel)

    def win_valid(c, k_rel):
        dist = t_rel - k_rel
        return (dist >= 0) & (dist < WINDOW)

    init = tuple((jnp.full((TQ, 1), NEG, F32), jnp.zeros((TQ, 1), F32), jnp.zeros((TQ, HEAD_DIM), F32))
                 for _ in range(GROUP))
    res_s = lax.fori_loop(0, i + 1, lambda c, carry: chunk_update(kvs_ref, c, carry, sel_valid), init)
    c_lo = jnp.maximum(i - (WINDOW // KC), 0)
    res_w = lax.fori_loop(c_lo, i + 1, lambda c, carry: chunk_update(kvw_ref, c, carry, win_valid), init)

    gb = gb_ref[...]
    for r in range(GROUP):
        gate = [gb[:, br * GROUP + r:br * GROUP + r + 1] for br in range(3)]
        o_s = res_s[r][2] * (1.0 / jnp.maximum(res_s[r][1], 1e-30))
        o_w = res_w[r][2] * (1.0 / jnp.maximum(res_w[r][1], 1e-30))
        o_ref[:, r * HEAD_DIM:(r + 1) * HEAD_DIM] = gate[0] * o_c[r] + gate[1] * o_s + gate[2] * o_w


def _prompt_attn(q, pp, kvs, kvw, gb, hc, w2, b2, slopes, batch):
    t = q.shape[0]
    seq = t // batch
    nt = seq // TQ
    n_ch = pp.shape[1]
    n_blk = n_ch - CMP_PARTS + 1
    ov = _overlap_matrix(n_blk, n_ch, LANES)
    gw = GROUP * HEAD_DIM
    return pl.pallas_call(
        functools.partial(_prompt_attn_kernel, n_blk=n_blk),
        grid=(batch, N_KV, nt),
        in_specs=[pl.BlockSpec(memory_space=pltpu.SMEM),
                  pl.BlockSpec((TQ, gw), lambda b, g, i: (b * nt + i, g)),
                  pl.BlockSpec((None, n_ch, 4 * CMP_HIDDEN), lambda b, g, i: (b, 0, g)),
                  pl.BlockSpec((None, seq, 2 * HEAD_DIM), lambda b, g, i: (b, 0, g)),
                  pl.BlockSpec((None, seq, 2 * HEAD_DIM), lambda b, g, i: (b, 0, g)),
                  pl.BlockSpec((TQ, LANES), lambda b, g, i: (b * nt + i, g)),
                  _full(hc.shape), _full(w2.shape), _full(b2.shape), _full(ov.shape)],
        out_specs=pl.BlockSpec((TQ, gw), lambda b, g, i: (b * nt + i, g)),
        out_shape=jax.ShapeDtypeStruct((t, D_MODEL), F32),
        scratch_shapes=[pltpu.VMEM((n_ch, HEAD_DIM), BF16), pltpu.VMEM((n_ch, HEAD_DIM), BF16)],
        compiler_params=_params("parallel", "parallel", "arbitrary"),
        name="prompt_attn",
    )(slopes, q, pp, kvs.reshape(batch, seq, KV_ROW), kvw.reshape(batch, seq, KV_ROW), gb, hc, w2, b2, ov)


def _stack_heads(q, g0):
    return jnp.concatenate([q[:, (g0 + r) * HEAD_DIM:(g0 + r + 1) * HEAD_DIM] for r in range(GROUP)],
                           axis=0).astype(BF16)


def _row_t(t):
    return jnp.concatenate([lax.broadcasted_iota(jnp.int32, (t, 1), 0)] * GROUP, axis=0)


def _row_slopes(slopes_ref, g, t):
    row = lax.broadcasted_iota(jnp.int32, (GROUP * t, 1), 0)
    out = jnp.zeros((GROUP * t, 1), F32)
    for r in range(GROUP):
        out = jnp.where((row >= r * t) & (row < (r + 1) * t), slopes_ref[g * GROUP + r], out)
    return out


def _sample_cmp_kernel(slopes_ref, q_ref, p_ref, hc_ref, w2_ref, b2_ref, ov_ref, oc_ref, sel_ref,
                       *, n_blk, past_len, blk_step):
    g = pl.program_id(1)
    t = q_ref.shape[0]
    kc, vc = _compress_finish(p_ref[...], hc_ref, w2_ref, b2_ref)
    qs = _stack_heads(q_ref[...], 0)
    n_pad = kc.shape[0]
    n_idx = lax.broadcasted_iota(jnp.int32, (1, n_pad), 1)
    end_rel = n_idx * CMP_STRIDE + (CMP_LEN - 1) - past_len
    valid = (_row_t(t) >= end_rel) & (n_idx < n_blk)
    s = _dot_nt(qs, kc) + _row_slopes(slopes_ref, g, t) * end_rel.astype(F32)
    p = _softmax_rows(s, valid)
    o = _dot(p.astype(BF16), vc)
    p_sum = None
    for r in range(GROUP):
        oc_ref[:, r * HEAD_DIM:(r + 1) * HEAD_DIM] = o[r * t:(r + 1) * t, :]
        p_sum = p[0:t] if r == 0 else p_sum + p[r * t:(r + 1) * t]
    imp = _dot_split3(p_sum, ov_ref[...])
    q_pos = lax.broadcasted_iota(jnp.int32, (t, 1), 0) + past_len
    sel = _topk_mask(_block_scores(imp, q_pos), N_SEL)
    sel_ref[...] = jnp.zeros(sel_ref.shape, F32)
    for c in range(sel_ref.shape[0]):
        sel_ref[c, :, 0:blk_step] = sel[:, c * blk_step:(c + 1) * blk_step]


def _sample_cmp(q, pp, hc, w2, b2, slopes, batch, past_len):
    t = q.shape[0] // batch
    n_ch = pp.shape[1]
    n_blk = n_ch - CMP_PARTS + 1
    n_sel_blk = -(-(past_len + t) // SEL_BLOCK)
    blk_step = PAGES_PER_STEP * PAGE_SIZE // SEL_BLOCK
    n_chunk = -(-n_sel_blk // blk_step)
    ov = _overlap_matrix(n_blk, n_ch, -(-n_chunk * blk_step // LANES) * LANES)
    gw = GROUP * HEAD_DIM
    return pl.pallas_call(
        functools.partial(_sample_cmp_kernel, n_blk=n_blk, past_len=past_len, blk_step=blk_step),
        grid=(batch, N_KV),
        in_specs=[pl.BlockSpec(memory_space=pltpu.SMEM),
                  pl.BlockSpec((t, gw), lambda b, g: (b, g)),
                  pl.BlockSpec((None, n_ch, 4 * CMP_HIDDEN), lambda b, g: (b, 0, g)),
                  _full(hc.shape), _full(w2.shape), _full(b2.shape), _full(ov.shape)],
        out_specs=[pl.BlockSpec((t, gw), lambda b, g: (b, g)),
                   pl.BlockSpec((None, None, n_chunk, t, LANES), lambda b, g: (b, g, 0, 0, 0))],
        out_shape=[jax.ShapeDtypeStruct((batch * t, D_MODEL), F32),
                   jax.ShapeDtypeStruct((batch, N_KV, n_chunk, t, LANES), F32)],
        compiler_params=_params("parallel", "parallel"),
        name="sample_cmp",
    )(slopes, q, pp, hc, w2, b2, ov)


def _sample_attn_kernel(*refs, past_len):
    n_pg = PAGES_PER_STEP
    pg_refs = refs[1:1 + n_pg]
    (slopes_ref, q_ref, selc_ref, sell_ref, ex_ref, kvs_ref, kvw_ref, win_ref, oc_ref, gb_ref,
     o_ref, nwin_ref, m_ref, l_ref, acc_ref) = refs[1 + n_pg:]
    c = pl.program_id(1)
    n_step = pl.num_programs(1)
    t = q_ref.shape[0]
    rows = GROUP * t
    step_keys = n_pg * PAGE_SIZE
    row_t = _row_t(t)

    @pl.when(c == 0)
    def _():
        m_ref[...] = jnp.full(m_ref.shape, NEG, F32)
        l_ref[...] = jnp.zeros(l_ref.shape, F32)
        acc_ref[...] = jnp.zeros(acc_ref.shape, F32)

    q = q_ref[...]
    k_rel = lax.broadcasted_iota(jnp.int32, (1, step_keys), 1) + (c * step_keys - past_len)
    k_rel_f = k_rel.astype(F32)
    for g in range(N_KV):
        kv = jnp.concatenate([pr[:, g * 2 * HEAD_DIM:(g + 1) * 2 * HEAD_DIM] for pr in pg_refs], axis=0)
        k = kv[:, 0:HEAD_DIM].astype(BF16)
        v = kv[:, HEAD_DIM:2 * HEAD_DIM].astype(BF16)
        qs = _stack_heads(q, g * GROUP)
        sel = jnp.concatenate([selc_ref[g]] * GROUP, axis=0).astype(BF16)
        valid = (_dot(sel, ex_ref[...]) > 0.5) & (k_rel <= row_t)
        s = _dot_nt(qs, k) + _row_slopes(slopes_ref, g, t) * k_rel_f
        m, l, acc = _online_update(s, valid, v, m_ref[g], l_ref[g], acc_ref[g])
        m_ref[g] = m
        l_ref[g] = l
        acc_ref[g] = acc

    @pl.when(c == n_step - 1)
    def _():
        wb = win_ref.shape[0]
        pad = jnp.zeros((LANES - t, KV_ROW), F32)
        new_s = jnp.concatenate([kvs_ref[...], pad], axis=0)
        new_w = jnp.concatenate([kvw_ref[...], pad], axis=0)
        i_new = lax.broadcasted_iota(jnp.int32, (1, LANES), 1)
        i_old = lax.broadcasted_iota(jnp.int32, (1, wb), 1) - wb
        gb = gb_ref[...]
        for g in range(N_KV):
            lo = g * 2 * HEAD_DIM
            qs = _stack_heads(q, g * GROUP)
            slope = _row_slopes(slopes_ref, g, t)
            sel_new = jnp.concatenate([sell_ref[g]] * GROUP, axis=0)[:, 0:1]
            valid = (sel_new > 0.5) & (i_new <= row_t) & (i_new < t)
            s = _dot_nt(qs, new_s[:, lo:lo + HEAD_DIM].astype(BF16)) + slope * i_new.astype(F32)
            m, l, acc = _online_update(s, valid, new_s[:, lo + HEAD_DIM:lo + 2 * HEAD_DIM].astype(BF16),
                                       m_ref[g], l_ref[g], acc_ref[g])
            o_s = acc * (1.0 / jnp.maximum(l, 1e-30))
            kv_old = win_ref[:, lo:lo + 2 * HEAD_DIM]
            dist = row_t - i_old
            valid = (dist >= 0) & (dist < WINDOW) & (i_old + past_len >= 0)
            s = _dot_nt(qs, kv_old[:, 0:HEAD_DIM].astype(BF16)) + slope * i_old.astype(F32)
            init = (jnp.full((rows, 1), NEG, F32), jnp.zeros((rows, 1), F32), jnp.zeros((rows, HEAD_DIM), F32))
            m, l, acc = _online_update(s, valid, kv_old[:, HEAD_DIM:2 * HEAD_DIM].astype(BF16), *init)
            dist = row_t - i_new
            valid = (dist >= 0) & (dist < WINDOW) & (i_new < t)
            s = _dot_nt(qs, new_w[:, lo:lo + HEAD_DIM].astype(BF16)) + slope * i_new.astype(F32)
            m, l, acc = _online_update(s, valid, new_w[:, lo + HEAD_DIM:lo + 2 * HEAD_DIM].astype(BF16), m, l, acc)
            o_w = acc * (1.0 / jnp.maximum(l, 1e-30))
            for r in range(GROUP):
                h = g * GROUP + r
                gate = [gb[:, g * LANES + br * GROUP + r:g * LANES + br * GROUP + r + 1] for br in range(3)]
                o_ref[:, h * HEAD_DIM:(h + 1) * HEAD_DIM] = (
                    gate[0] * oc_ref[:, h * HEAD_DIM:(h + 1) * HEAD_DIM]
                    + gate[1] * o_s[r * t:(r + 1) * t, :] + gate[2] * o_w[r * t:(r + 1) * t, :])
        nwin_ref[0:wb - t, :] = win_ref[t:wb, :]
        nwin_ref[wb - t:wb, :] = kvw_ref[...]


def _sample_attn(q, selc, kvs, kvw, cache_sel, state_win, page_table, o_c, gb, slopes, layer, past_len):
    batch, n_pages = page_table.shape
    t = q.shape[0] // batch
    n_step = n_pages // PAGES_PER_STEP
    step_keys = PAGES_PER_STEP * PAGE_SIZE
    wb = state_win.shape[2]
    x = cache_sel.reshape(cache_sel.shape[0], cache_sel.shape[1], PAGE_SIZE, KV_ROW)
    win = state_win.reshape(state_win.shape[0], batch, wb, KV_ROW)
    jj = np.arange(LANES)[:, None]
    kk = np.arange(step_keys)[None, :]
    expand = jnp.asarray(jj == kk // SEL_BLOCK, dtype=BF16)
    n_chunk = selc.shape[2]

    def page_spec(k):
        return pl.BlockSpec((None, None, PAGE_SIZE, KV_ROW),
                            lambda b, c, pt: (layer, pt[b, c * PAGES_PER_STEP + k], 0, 0))

    tok = lambda n: pl.BlockSpec((t, n), lambda b, c, pt: (b, 0))
    return pl.pallas_call(
        functools.partial(_sample_attn_kernel, past_len=past_len),
        grid_spec=pltpu.PrefetchScalarGridSpec(
            num_scalar_prefetch=1,
            grid=(batch, n_step),
            in_specs=[page_spec(k) for k in range(PAGES_PER_STEP)] + [
                pl.BlockSpec(memory_space=pltpu.SMEM),
                tok(D_MODEL),
                pl.BlockSpec((None, N_KV, None, t, LANES), lambda b, c, pt: (b, 0, c, 0, 0)),
                pl.BlockSpec((None, N_KV, None, t, LANES), lambda b, c, pt: (b, 0, n_chunk - 1, 0, 0)),
                pl.BlockSpec(expand.shape, lambda b, c, pt: (0, 0)),
                tok(KV_ROW), tok(KV_ROW),
                pl.BlockSpec((None, None, wb, KV_ROW), lambda b, c, pt: (layer, b, 0, 0)),
                tok(D_MODEL), tok(N_KV * LANES)],
            out_specs=[tok(D_MODEL), pl.BlockSpec((None, wb, KV_ROW), lambda b, c, pt: (b, 0, 0))],
            scratch_shapes=[pltpu.VMEM((N_KV, GROUP * t, 1), F32), pltpu.VMEM((N_KV, GROUP * t, 1), F32),
                            pltpu.VMEM((N_KV, GROUP * t, HEAD_DIM), F32)],
        ),
        out_shape=[jax.ShapeDtypeStruct((batch * t, D_MODEL), F32),
                   jax.ShapeDtypeStruct((batch, wb, KV_ROW), F32)],
        compiler_params=_params("parallel", "arbitrary"),
        name="sample_attn",
    )(page_table, *([x] * PAGES_PER_STEP), slopes, q, selc, selc, expand, kvs, kvw, win, o_c, gb)


HIST = 32


def _conv_kernel(z_ref, prev_ref, hist_ref, w_ref, b_ref, y_ref, st_ref, zc_ref):
    i = pl.program_id(1)
    tm = z_ref.shape[0]

    @pl.when(i == 0)
    def _():
        zc_ref[0:HIST, :] = hist_ref[...]

    @pl.when(i > 0)
    def _():
        zc_ref[0:HIST, :] = prev_ref[...]

    zc_ref[HIST:HIST + tm, :] = z_ref[...]
    off = HIST - (CONV_W - 1)
    for cb in range(C_CONV // LANES):
        cols = slice(cb * LANES, (cb + 1) * LANES)
        acc = jnp.zeros((tm, LANES), F32) + b_ref[:, cols]
        for k in range(CONV_W):
            acc = acc + zc_ref[off + k:off + k + tm, cols] * w_ref[k:k + 1, cols]
        y_ref[:, cols] = acc

    @pl.when(i == pl.num_programs(1) - 1)
    def _():
        st_ref[...] = zc_ref[HIST + tm - (CONV_W - 1):HIST + tm, :]


def _conv(z, hist, w, b, batch, tm):
    seq = z.shape[0] // batch
    nt = seq // tm
    per = tm // HIST if tm >= HIST else 1
    z3 = z.reshape(batch, seq, C_CONV)
    if tm >= HIST:
        prev_spec = pl.BlockSpec((None, HIST, C_CONV), lambda b_, i: (b_, jnp.maximum(i * per - 1, 0), 0))
        prev = z3
    else:
        prev_spec = pl.BlockSpec((None, HIST, C_CONV), lambda b_, i: (b_, 0, 0))
        prev = hist
    y, st = pl.pallas_call(
        _conv_kernel,
        grid=(batch, nt),
        in_specs=[pl.BlockSpec((None, tm, C_CONV), lambda b_, i: (b_, i, 0)),
                  prev_spec,
                  pl.BlockSpec((None, HIST, C_CONV), lambda b_, i: (b_, 0, 0)),
                  _full(w.shape), _full(b.shape)],
        out_specs=[pl.BlockSpec((None, tm, C_CONV), lambda b_, i: (b_, i, 0)),
                   pl.BlockSpec((None, CONV_W - 1, C_CONV), lambda b_, i: (b_, 0, 0))],
        out_shape=[jax.ShapeDtypeStruct((batch, seq, C_CONV), F32),
                   jax.ShapeDtypeStruct((batch, CONV_W - 1, C_CONV), F32)],
        scratch_shapes=[pltpu.VMEM((HIST + tm, C_CONV), F32)],
        compiler_params=_params("parallel", "arbitrary"),
        name="conv",
    )(z3, prev, hist, w, b)
    return y.reshape(batch * seq, C_CONV), st


def _post_kernel(x_ref, y_ref, oa_ref, gm_ref, ple_ref,
                 wpw_ref, wout_ref, wg_ref, wu_ref, wd_ref, wpg_ref, wpp_ref,
                 cg_ref, cb_ref, bpw_ref, g1_ref, b1_ref, g2_ref, b2_ref, o_ref, *, alpha):
    yn = _layernorm(y_ref[...], cg_ref[...], cb_ref[...])
    o_b = _dot((yn * _sigmoid(yn)).astype(BF16), wpw_ref[...]) + bpw_ref[...]
    mixed = gm_ref[:, 0:D_MODEL] * oa_ref[...] + gm_ref[:, D_MODEL:2 * D_MODEL] * o_b
    x1 = _layernorm(alpha * x_ref[...] + _dot(mixed.astype(BF16), wout_ref[...]), g1_ref[...], b1_ref[...])
    x1b = x1.astype(BF16)
    hg = _dot(x1b, wg_ref[...])
    hu = _dot(x1b, wu_ref[...])
    f = _dot((hg * _sigmoid(hg) * hu).astype(BF16), wd_ref[...])
    x2 = _layernorm(alpha * x1 + f, g2_ref[...], b2_ref[...])
    gate = _sigmoid(_dot(x2.astype(BF16), wpg_ref[...]))
    o_ref[...] = x2 + gate * _dot(ple_ref[...].astype(BF16), wpp_ref[...])


def _post(x, y, o_a, gm, ple, w, alpha, tm):
    t = x.shape[0]
    row = lambda n: pl.BlockSpec((tm, n), lambda i: (i, 0))
    mats = [w["w_pw2"], w["w_out"], w["w_ffn_gate"], w["w_ffn_up"], w["w_ffn_down"], w["w_ple_gate"], w["w_ple_in"]]
    vecs = [w["conv_ln_g"], w["conv_ln_b"], w["b_pw2"], w["ln1_g"], w["ln1_b"], w["ln2_g"], w["ln2_b"]]
    return pl.pallas_call(
        functools.partial(_post_kernel, alpha=alpha),
        grid=(t // tm,),
        in_specs=[row(D_MODEL), row(C_CONV), row(D_MODEL), row(2 * D_MODEL), row(ple.shape[1])]
                 + [_resident(m.shape) for m in mats] + [_resident(v.shape) for v in vecs],
        out_specs=row(D_MODEL),
        out_shape=jax.ShapeDtypeStruct((t, D_MODEL), F32),
        compiler_params=_params("parallel"),
        name="post",
    )(x, y, o_a, gm, ple, *mats, *vecs)


def _prep_layer(l, w_in, b_in, cmp_w1, cmp_w2, cmp_b2, named):
    offs = np.cumsum((D_MODEL, 3 * KV_ROW, 3 * N_HEADS, 2 * C_CONV, 2 * D_MODEL))
    wl, bl = w_in[l], b_in[l]
    new_cols = np.array([g * LANES + br * GROUP + r for br in range(3) for g in range(N_KV) for r in range(GROUP)])
    wgb = jnp.zeros((D_MODEL, N_KV * LANES), F32).at[:, new_cols].set(wl[:, offs[1]:offs[2]])
    bgb = jnp.zeros((N_KV * LANES,), F32).at[new_cols].set(bl[offs[1]:offs[2]])
    out = {
        "wq": (wl[:, :offs[0]] * ATTN_SCALE).astype(BF16), "bq": (bl[:offs[0]] * ATTN_SCALE)[None],
        "wkv": wl[:, offs[0]:offs[1]].astype(BF16), "bkv": bl[offs[0]:offs[1]][None],
        "wgb": wgb.astype(BF16), "bgb": bgb[None],
        "wu": wl[:, offs[2]:offs[3]].astype(BF16), "bu": bl[offs[2]:offs[3]][None],
        "wgm": wl[:, offs[3]:offs[4]].astype(BF16), "bgm": bl[offs[3]:offs[4]][None],
    }
    w1 = cmp_w1[l].reshape(2, CMP_PARTS, CMP_STRIDE, HEAD_DIM, CMP_HIDDEN).transpose(2, 0, 3, 1, 4)
    zero = jnp.zeros_like(w1[:, 0])
    w1cat = jnp.stack([jnp.stack([w1[:, 0], zero], axis=2), jnp.stack([zero, w1[:, 1]], axis=2)], axis=1)
    out["w1cat"] = w1cat.reshape(CMP_STRIDE, 2 * HEAD_DIM, 4 * CMP_HIDDEN).astype(BF16)
    out["w2"] = cmp_w2[l].astype(BF16)
    out["b2"] = cmp_b2[l]
    for name, val in named.items():
        v = val[l]
        out[name] = v.astype(BF16) if v.ndim == 2 else v[None]
    return out


def kernel(x_prompt, x_sample, cache_cmp, cache_sel, state_win, state_conv, page_table, p_prompt, p_sample,
           w_in, b_in, cmp_w1, cmp_pe, cmp_b1, cmp_w2, cmp_b2, conv_w, conv_b, conv_ln_g, conv_ln_b,
           w_pw2, b_pw2, w_out, ln1_g, ln1_b, w_ffn_gate, w_ffn_up, w_ffn_down, ln2_g, ln2_b,
           w_ple_in, w_ple_gate):
    depth = w_in.shape[0]
    alpha = float((2 * depth) ** 0.25)
    batch, seq = x_prompt.shape[:2]
    dbatch, dseq = x_sample.shape[:2]
    past_len = page_table.shape[1] * PAGE_SIZE
    slopes = jnp.exp2(-8.0 * jnp.arange(1, N_HEADS + 1, dtype=F32) / N_HEADS)
    hconst = _pe_term(cmp_pe, cmp_w1, cmp_b1)
    named = dict(w_pw2=w_pw2, w_out=w_out, w_ffn_gate=w_ffn_gate, w_ffn_up=w_ffn_up, w_ffn_down=w_ffn_down,
                 w_ple_gate=w_ple_gate, w_ple_in=w_ple_in, conv_ln_g=conv_ln_g, conv_ln_b=conv_ln_b,
                 b_pw2=b_pw2, ln1_g=ln1_g, ln1_b=ln1_b, ln2_g=ln2_g, ln2_b=ln2_b, conv_b=conv_b)
    layers = [_prep_layer(l, w_in, b_in, cmp_w1, cmp_w2, cmp_b2, named) for l in range(depth)]

    x = x_prompt.reshape(batch * seq, D_MODEL)
    zero_hist = jnp.zeros((batch, HIST, C_CONV), F32)
    out_p = [[], [], [], []]
    for l, w in enumerate(layers):
        q, kvc, kvs, kvw, gb, z, gm = _in_proj(x, w, BF16, 256)
        pp = _cmp_partial_prompt(kvc, w["w1cat"], batch)
        o_a = _prompt_attn(q, pp, kvs, kvw, gb, hconst[l], w["w2"], w["b2"], slopes, batch)
        y, st = _conv(z, zero_hist, conv_w[l], w["conv_b"], batch, 256)
        x = _post(x, y, o_a, gm, p_prompt[l].reshape(batch * seq, -1), w, alpha, 256)
        wn = min(WINDOW, seq)
        out_p[0].append(kvc.reshape(batch, seq, N_KV, 2, HEAD_DIM))
        out_p[1].append(kvs.reshape(batch, seq, N_KV, 2, HEAD_DIM))
        out_p[2].append(kvw.reshape(batch, seq, N_KV, 2, HEAD_DIM)[:, seq - wn:])
        out_p[3].append(st)
    y_prompt = x.reshape(batch, seq, D_MODEL)

    x = x_sample.reshape(dbatch * dseq, D_MODEL)
    out_s = [[], [], [], []]
    for l, w in enumerate(layers):
        q, kvc, kvs, kvw, gb, z, gm = _in_proj(x, w, F32, dbatch * dseq)
        pp = _cmp_partial_sample(cache_cmp, page_table, w["w1cat"], l)
        o_c, selc = _sample_cmp(q, pp, hconst[l], w["w2"], w["b2"], slopes, dbatch, past_len)
        o_a, nwin = _sample_attn(q, selc, kvs, kvw, cache_sel, state_win, page_table, o_c, gb, slopes, l, past_len)
        hist = jnp.pad(state_conv[l], ((0, 0), (HIST - (CONV_W - 1), 0), (0, 0)))
        y, st = _conv(z, hist, conv_w[l], w["conv_b"], dbatch, dseq)
        x = _post(x, y, o_a, gm, p_sample[l].reshape(dbatch * dseq, -1), w, alpha, dbatch * dseq)
        out_s[0].append(kvc.reshape(dbatch, dseq, N_KV, 2, HEAD_DIM))
        out_s[1].append(kvs.reshape(dbatch, dseq, N_KV, 2, HEAD_DIM))
        out_s[2].append(nwin.reshape(dbatch, -1, N_KV, 2, HEAD_DIM))
        out_s[3].append(st)
    y_sample = x.reshape(dbatch, dseq, D_MODEL)

    return (y_prompt, y_sample, jnp.stack(out_p[0]), jnp.stack(out_p[1]), jnp.stack(out_p[2]), jnp.stack(out_p[3]),
            jnp.stack(out_s[0]), jnp.stack(out_s[1]), jnp.stack(out_s[2]), jnp.stack(out_s[3]))
```

```python
import functools
import math

import numpy as np
import jax
import jax.numpy as jnp
from jax import lax
from jax.experimental import pallas as pl
from jax.experimental.pallas import tpu as pltpu

F32 = jnp.float32
BF16 = jnp.bfloat16

D_MODEL = 1024
N_HEADS = 16
HEAD_DIM = 64
N_KV = 4
GROUP = N_HEADS // N_KV
KV_ROW = N_KV * 2 * HEAD_DIM
GKV = 2 * HEAD_DIM
CMP_LEN = 32
CMP_STRIDE = 16
CMP_PARTS = CMP_LEN // CMP_STRIDE
CMP_HIDDEN = 2 * HEAD_DIM
SEL_BLOCK = 64
SEL_SHIFT = SEL_BLOCK.bit_length() - 1
N_SEL = 16
WINDOW = 512
C_CONV = D_MODEL
CONV_W = 31
PAGE_SIZE = 128
CH_PAGE = PAGE_SIZE // CMP_STRIDE
LN_EPS = 1e-5
ATTN_SCALE = HEAD_DIM ** -0.5
LOG2E = math.log2(math.e)
FORCE_SCORE = 1e4
NEG = -1e30
LANES = 128
SUBLANES = 8
VMEM_LIMIT = 56 * 1024 * 1024

PAGES_PER_STEP = 16
TQ = 256
KS = 512
RS = TQ


def _dot(a, b):
    return jnp.dot(a, b, preferred_element_type=F32)


def _dot_nt(a, b):
    return lax.dot_general(a, b, (((1,), (1,)), ((), ())), preferred_element_type=F32)


def _split3(a):
    a1 = a.astype(BF16)
    r1 = a - a1.astype(F32)
    a2 = r1.astype(BF16)
    return a1, a2, (r1 - a2.astype(F32)).astype(BF16)


def _sigmoid(x):
    return 1.0 / (1.0 + jnp.exp(-x))


def _layernorm(x, g, b):
    mu = jnp.mean(x, axis=-1, keepdims=True)
    xc = x - mu
    var = jnp.mean(xc * xc, axis=-1, keepdims=True)
    return xc * lax.rsqrt(var + LN_EPS) * g + b


def _params(*sem):
    return pltpu.CompilerParams(dimension_semantics=sem, vmem_limit_bytes=VMEM_LIMIT)


def _full(shape):
    n = len(shape)
    return pl.BlockSpec(shape, lambda *_: (0,) * n)


def _resident(shape):
    n = len(shape)
    return pl.BlockSpec(shape, lambda *_: (0,) * n, pipeline_mode=pl.Buffered(1))


def _pe_term_kernel(pe_ref, w1_ref, b1_ref, o_ref):
    o_ref[...] = b1_ref[...] + jnp.sum(pe_ref[...] * w1_ref[...], axis=0, keepdims=True)


def _pe_term(cmp_pe, cmp_w1, cmp_b1):
    n = cmp_pe.shape[0] * 2
    k = CMP_LEN * HEAD_DIM
    out = pl.pallas_call(
        _pe_term_kernel,
        grid=(n,),
        in_specs=[pl.BlockSpec((None, k, 1), lambda i: (i, 0, 0)),
                  pl.BlockSpec((None, k, CMP_HIDDEN), lambda i: (i, 0, 0)),
                  pl.BlockSpec((None, 1, CMP_HIDDEN), lambda i: (i, 0, 0))],
        out_specs=pl.BlockSpec((None, 1, CMP_HIDDEN), lambda i: (i, 0, 0)),
        out_shape=jax.ShapeDtypeStruct((n, 1, CMP_HIDDEN), F32),
        compiler_params=_params("parallel"),
        name="pe_term",
    )(cmp_pe.reshape(n, k, 1), cmp_w1.reshape(n, k, CMP_HIDDEN), cmp_b1.reshape(n, 1, CMP_HIDDEN))
    return out.reshape(-1, 2, CMP_HIDDEN)


def _in_proj_kernel(x_ref, wq_ref, wkv_ref, wgb_ref, wu_ref, wgm_ref,
                    bq_ref, bkv_ref, bgb_ref, bu_ref, bgm_ref,
                    q_ref, kvc_ref, kvs_ref, kvw_ref, gb_ref, z_ref, gm_ref, *, kv_t):
    xb = x_ref[...].astype(BF16)
    q_ref[...] = (_dot(xb, wq_ref[...]) + bq_ref[...]).astype(q_ref.dtype)
    if kv_t:
        kv = _dot_nt(wkv_ref[...], xb) + bkv_ref[...]
        kvc_ref[...] = kv[0:KV_ROW, :]
        kvs_ref[...] = kv[KV_ROW:2 * KV_ROW, :]
        kvw_ref[...] = kv[2 * KV_ROW:3 * KV_ROW, :]
    else:
        kv = _dot_nt(xb, wkv_ref[...]) + bkv_ref[...]
        kvc_ref[...] = kv[:, 0:KV_ROW]
        kvs_ref[...] = kv[:, KV_ROW:2 * KV_ROW]
        kvw_ref[...] = kv[:, 2 * KV_ROW:3 * KV_ROW]
    gb_ref[...] = _sigmoid(_dot(xb, wgb_ref[...]) + bgb_ref[...])
    u = _dot(xb, wu_ref[...]) + bu_ref[...]
    z_ref[...] = u[:, 0:C_CONV] * _sigmoid(u[:, C_CONV:2 * C_CONV])
    gm_ref[...] = _sigmoid(_dot(xb, wgm_ref[...]) + bgm_ref[...])


def _in_proj(x, w, batch, q_dtype, tm, kv_t):
    t = x.shape[0]
    nt = t // batch // tm
    row = lambda n: pl.BlockSpec((tm, n), lambda b, i: (b * nt + i, 0))
    mats = [w["wq"], w["wkvT"], w["wgb"], w["wu"], w["wgm"]]
    vecs = [w["bq"], w["bkv_col"] if kv_t else w["bkv_row"], w["bgb"], w["bu"], w["bgm"]]
    if kv_t:
        kv_spec = pl.BlockSpec((None, KV_ROW, tm), lambda b, i: (b, 0, i))
        kv_shape = jax.ShapeDtypeStruct((batch, KV_ROW, t // batch), F32)
    else:
        kv_spec = row(KV_ROW)
        kv_shape = jax.ShapeDtypeStruct((t, KV_ROW), F32)
    return pl.pallas_call(
        functools.partial(_in_proj_kernel, kv_t=kv_t),
        grid=(batch, nt),
        in_specs=[row(D_MODEL)] + [_resident(m.shape) for m in mats] + [_resident(v.shape) for v in vecs],
        out_specs=[row(D_MODEL), kv_spec, kv_spec, kv_spec, row(N_KV * LANES), row(C_CONV), row(2 * D_MODEL)],
        out_shape=[jax.ShapeDtypeStruct((t, D_MODEL), q_dtype), kv_shape, kv_shape, kv_shape,
                   jax.ShapeDtypeStruct((t, N_KV * LANES), F32),
                   jax.ShapeDtypeStruct((t, C_CONV), F32),
                   jax.ShapeDtypeStruct((t, 2 * D_MODEL), F32)],
        compiler_params=_params("parallel", "parallel"),
        name="in_proj",
    )(x, *mats, *vecs)


def _cmp_partial_body(page_fn, n_pages, perm_ref, w_ref, p_ref):
    perm = perm_ref[...]
    regrouped = [[_dot_nt(perm, page_fn(k, g).astype(BF16)) for k in range(n_pages)] for g in range(N_KV)]
    a = jnp.concatenate(
        [jnp.concatenate([regrouped[g][k][j * CH_PAGE:(j + 1) * CH_PAGE, :]
                          for g in range(N_KV) for k in range(n_pages)], axis=0).astype(BF16)
         for j in range(CMP_STRIDE)], axis=1)
    acc = _dot(a, w_ref[...])
    rows = n_pages * CH_PAGE
    for g in range(N_KV):
        p_ref[:, g * 4 * CMP_HIDDEN:(g + 1) * 4 * CMP_HIDDEN] = acc[g * rows:(g + 1) * rows, :]


def _cmp_partial_prompt_kernel(x_ref, perm_ref, w_ref, p_ref):
    n_pages = x_ref.shape[1] // PAGE_SIZE
    page_fn = lambda k, g: x_ref[g * GKV:(g + 1) * GKV, k * PAGE_SIZE:(k + 1) * PAGE_SIZE]
    _cmp_partial_body(page_fn, n_pages, perm_ref, w_ref, p_ref)


def _cmp_partial_sample_kernel(*refs):
    pg_refs = refs[1:1 + PAGES_PER_STEP]
    perm_ref, w_ref, p_ref = refs[1 + PAGES_PER_STEP:]
    page_fn = lambda k, g: pg_refs[k][g].reshape(GKV, PAGE_SIZE)
    _cmp_partial_body(page_fn, PAGES_PER_STEP, perm_ref, w_ref, p_ref)


def _row_regroup_matrix():
    out = np.zeros((PAGE_SIZE, PAGE_SIZE), np.float32)
    for j in range(CMP_STRIDE):
        for c in range(CH_PAGE):
            out[j * CH_PAGE + c, c * CMP_STRIDE + j] = 1.0
    return jnp.asarray(out, dtype=BF16)


def _cmp_partial_prompt(kvc_t, w1cat):
    batch, _, seq = kvc_t.shape
    n_ch = seq // CMP_STRIDE
    perm = _row_regroup_matrix()
    return pl.pallas_call(
        _cmp_partial_prompt_kernel,
        grid=(batch,),
        in_specs=[pl.BlockSpec((None, KV_ROW, seq), lambda b: (b, 0, 0)),
                  _resident(perm.shape), _resident(w1cat.shape)],
        out_specs=pl.BlockSpec((None, n_ch, N_KV * 4 * CMP_HIDDEN), lambda b: (b, 0, 0)),
        out_shape=jax.ShapeDtypeStruct((batch, n_ch, N_KV * 4 * CMP_HIDDEN), F32),
        compiler_params=_params("parallel"),
        name="cmp_partial_prompt",
    )(kvc_t, perm, w1cat)


def _paged_view(cache):
    return jnp.transpose(cache, (0, 1, 3, 4, 5, 2))


def _page_spec(layer, k):
    return pl.BlockSpec((None, None, N_KV, 2, HEAD_DIM, PAGE_SIZE),
                        lambda b, c, pt: (layer, pt[b, c * PAGES_PER_STEP + k], 0, 0, 0, 0))


def _cmp_partial_sample(cache_t, page_table, w1cat, layer):
    batch, n_pages = page_table.shape
    n_step = n_pages // PAGES_PER_STEP
    rows = PAGES_PER_STEP * CH_PAGE
    perm = _row_regroup_matrix()
    const = lambda shape: pl.BlockSpec(shape, lambda b, c, pt: (0,) * len(shape), pipeline_mode=pl.Buffered(1))
    return pl.pallas_call(
        _cmp_partial_sample_kernel,
        grid_spec=pltpu.PrefetchScalarGridSpec(
            num_scalar_prefetch=1,
            grid=(batch, n_step),
            in_specs=[_page_spec(layer, k) for k in range(PAGES_PER_STEP)] + [const(perm.shape), const(w1cat.shape)],
            out_specs=pl.BlockSpec((None, rows, N_KV * 4 * CMP_HIDDEN), lambda b, c, pt: (b, c, 0)),
        ),
        out_shape=jax.ShapeDtypeStruct((batch, n_step * rows, N_KV * 4 * CMP_HIDDEN), F32),
        compiler_params=_params("parallel", "parallel"),
        name="cmp_partial_sample",
    )(page_table, *([cache_t] * PAGES_PER_STEP), perm, w1cat)


def _compress_finish(pb, hc_ref, w2t_ref, b2_ref):
    n = pb.shape[0]
    out = []
    for s in range(2):
        p0 = pb[:, s * 2 * CMP_HIDDEN:s * 2 * CMP_HIDDEN + CMP_HIDDEN]
        p1 = pb[:, s * 2 * CMP_HIDDEN + CMP_HIDDEN:(s + 1) * 2 * CMP_HIDDEN]
        h = hc_ref[s:s + 1, :] + p0 + pltpu.roll(p1, n - 1, 0)
        a = h * _sigmoid(h)
        out.append((_dot_nt(w2t_ref[s], a.astype(BF16)) + b2_ref[s]).astype(BF16))
    return out


def _softmax_rows(s, valid):
    s = jnp.where(valid, s, NEG)
    m = jnp.max(s, axis=-1, keepdims=True)
    e = jnp.where(valid, jnp.exp2(s - m), 0.0)
    return e * (1.0 / jnp.maximum(jnp.sum(e, axis=-1, keepdims=True), 1e-30))


def _topk_mask(score, n_top, axis):
    idx = lax.broadcasted_iota(jnp.int32, score.shape, axis).astype(F32)
    big = float(score.shape[axis])
    sel = jnp.zeros(score.shape, F32)
    for _ in range(n_top):
        m = jnp.max(score, axis=axis, keepdims=True)
        first = jnp.min(jnp.where(score == m, idx, big), axis=axis, keepdims=True)
        hit = idx == first
        sel = jnp.where(hit & (m > -0.5), 1.0, sel)
        score = jnp.where(hit, -3e38, score)
    return sel


def _block_scores(imp, q_pos, axis):
    j = lax.broadcasted_iota(jnp.int32, imp.shape, axis)
    cur = jnp.right_shift(q_pos, SEL_SHIFT)
    forced = (j == 0) | (j == cur) | (j == cur - 1)
    return jnp.where(j > cur, -1.0, jnp.where(forced, FORCE_SCORE, imp))


def _online_update(s, v_aug_t, m, acc):
    m_new = jnp.maximum(m, jnp.max(s, axis=-1, keepdims=True))
    p = jnp.exp2(s - m_new).astype(BF16)
    acc = jnp.exp2(m - m_new) * acc + _dot_nt(p, v_aug_t)
    return m_new, acc


def _finish(acc):
    return acc[:, 0:HEAD_DIM] * (1.0 / jnp.maximum(acc[:, HEAD_DIM:2 * HEAD_DIM], 1e-30))


def _overlap_matrix(n_blk, n_blk_pad, n_sel_pad, transpose):
    n = np.arange(n_blk_pad)[:, None]
    j = np.arange(n_sel_pad)[None, :]
    ov = (n * CMP_STRIDE < j * SEL_BLOCK + SEL_BLOCK) & (n * CMP_STRIDE + CMP_LEN > j * SEL_BLOCK) & (n < n_blk)
    return jnp.asarray(ov.T if transpose else ov, dtype=BF16)


def _block_expand_matrix(n_keys):
    jj = np.arange(LANES)[:, None]
    kk = np.arange(n_keys)[None, :]
    return jnp.asarray(jj == kk // SEL_BLOCK, dtype=BF16)


def _prompt_select_kernel(slopes_ref, q_ref, p_ref, gb_ref, hc_ref, w2t_ref, b2_ref, ovt_ref,
                          oc_ref, sel_ref, kc_ref, vc_ref, *, n_blk, n_sel_blk):
    i = pl.program_id(1)
    t0 = i * TQ

    @pl.when(i == 0)
    def _():
        for g in range(N_KV):
            kc, vc = _compress_finish(p_ref[:, g * 4 * CMP_HIDDEN:(g + 1) * 4 * CMP_HIDDEN],
                                      hc_ref, w2t_ref, b2_ref)
            kc_ref[g] = kc
            vc_ref[g] = vc

    q = q_ref[...]
    gb = gb_ref[...]
    t_rel = lax.broadcasted_iota(jnp.int32, (TQ, 1), 0)
    n_pad = kc_ref.shape[2]
    n_idx = lax.broadcasted_iota(jnp.int32, (1, n_pad), 1)
    end_rel = n_idx * CMP_STRIDE + (CMP_LEN - 1) - t0
    valid_c = (t_rel >= end_rel) & (n_idx < n_blk)
    end_rel_f = end_rel.astype(F32)
    ovt = ovt_ref[...]
    imp_parts = []
    for g in range(N_KV):
        kc = kc_ref[g]
        vc = vc_ref[g]
        p_sum = None
        for r in range(GROUP):
            h = g * GROUP + r
            s = _dot(q[:, h * HEAD_DIM:(h + 1) * HEAD_DIM], kc) + slopes_ref[h] * end_rel_f
            p = _softmax_rows(s, valid_c)
            gate = gb[:, g * LANES + r:g * LANES + r + 1]
            oc_ref[:, h * HEAD_DIM:(h + 1) * HEAD_DIM] = gate * _dot_nt(p.astype(BF16), vc)
            p_sum = p if p_sum is None else p_sum + p
        imp_parts.append(sum(_dot_nt(ovt, part) for part in _split3(p_sum))[0:n_sel_blk, :])
    imp_t = jnp.stack(imp_parts, axis=0)
    q_pos = lax.broadcasted_iota(jnp.int32, (1, 1, TQ), 2) + t0
    sel_t = _topk_mask(_block_scores(imp_t, q_pos, 1), N_SEL, 1).reshape(N_KV * n_sel_blk, TQ)
    sel_ref[...] = sel_t.T.astype(BF16)


def _prompt_select(q, pp, gb, hc, w2t, b2, slopes, batch):
    t = q.shape[0]
    seq = t // batch
    nt = seq // TQ
    n_ch = pp.shape[1]
    n_blk = n_ch - CMP_PARTS + 1
    n_sel_blk = -(-seq // SEL_BLOCK)
    assert N_KV * n_sel_blk == LANES
    ovt = _overlap_matrix(n_blk, n_ch, LANES, True)
    row = lambda n: pl.BlockSpec((TQ, n), lambda b, i: (b * nt + i, 0))
    return pl.pallas_call(
        functools.partial(_prompt_select_kernel, n_blk=n_blk, n_sel_blk=n_sel_blk),
        grid=(batch, nt),
        in_specs=[pl.BlockSpec(memory_space=pltpu.SMEM),
                  row(D_MODEL),
                  pl.BlockSpec((None, n_ch, N_KV * 4 * CMP_HIDDEN), lambda b, i: (b, 0, 0)),
                  row(N_KV * LANES),
                  _full(hc.shape), _full(w2t.shape), _full(b2.shape), _full(ovt.shape)],
        out_specs=[row(D_MODEL), row(LANES)],
        out_shape=[jax.ShapeDtypeStruct((t, D_MODEL), F32), jax.ShapeDtypeStruct((t, LANES), BF16)],
        scratch_shapes=[pltpu.VMEM((N_KV, HEAD_DIM, n_ch), BF16), pltpu.VMEM((N_KV, HEAD_DIM, n_ch), BF16)],
        compiler_params=_params("parallel", "arbitrary"),
        name="prompt_select",
    )(slopes, q, pp, gb, hc, w2t, b2, ovt)


WIN_SPAN = WINDOW + TQ


def _prompt_attn_kernel(slopes_ref, q_ref, sel_ref, kvs_ref, kvw_ref, gb_ref, oc_ref, ex_ref, wbias_ref, o_ref):
    g = pl.program_id(1)
    i = pl.program_id(2)
    t0 = i * TQ
    q = q_ref[...]
    q_heads = [q[:, r * HEAD_DIM:(r + 1) * HEAD_DIM] for r in range(GROUP)]
    slopes = [slopes_ref[g * GROUP + r] for r in range(GROUP)]
    t_rel = lax.broadcasted_iota(jnp.int32, (TQ, 1), 0)
    row_tiles = [slice(a, a + RS) for a in range(0, TQ, RS)]

    start = pl.multiple_of(jnp.maximum(t0 - WINDOW, 0), TQ)
    k_t = kvw_ref[0:HEAD_DIM, pl.ds(start, WIN_SPAN)].astype(BF16)
    v_aug_t = jnp.concatenate([kvw_ref[HEAD_DIM:GKV, pl.ds(start, WIN_SPAN)].astype(BF16),
                               jnp.ones((HEAD_DIM, WIN_SPAN), BF16)], axis=0)
    k_rel_f = (lax.broadcasted_iota(jnp.int32, (1, WIN_SPAN), 1) + (start - t0)).astype(F32)
    wbias = wbias_ref[...]
    o_w = []
    for r in range(GROUP):
        parts = []
        for rows in row_tiles:
            s = _dot(q_heads[r][rows], k_t) + (slopes[r] * k_rel_f + wbias[rows])
            p = jnp.exp2(s - jnp.max(s, axis=-1, keepdims=True)).astype(BF16)
            parts.append(_finish(_dot_nt(p, v_aug_t)))
        o_w.append(jnp.concatenate(parts, axis=0))

    sel = sel_ref[...]
    ones = jnp.ones((HEAD_DIM, KS), BF16)

    c_diag = lax.div(t0 + (TQ - 1), KS)

    def sel_step(n, carry):
        c = c_diag - n
        col = pl.ds(pl.multiple_of(c * KS, KS), KS)
        picked = _dot(sel, ex_ref[:, col]) > 0.5
        k_rel = lax.broadcasted_iota(jnp.int32, (1, KS), 1) + (c * KS - t0)
        bias = jnp.where(picked & (k_rel <= t_rel), 0.0, NEG)
        k_rel_f = k_rel.astype(F32)
        k_t = kvs_ref[0:HEAD_DIM, col].astype(BF16)
        v_aug_t = jnp.concatenate([kvs_ref[HEAD_DIM:GKV, col].astype(BF16), ones], axis=0)
        out = []
        for r in range(GROUP):
            m, acc = carry[r]
            parts = []
            for rows in row_tiles:
                s = _dot(q_heads[r][rows], k_t) + (slopes[r] * k_rel_f + bias[rows])
                parts.append(_online_update(s, v_aug_t, m[rows], acc[rows]))
            out.append((jnp.concatenate([p[0] for p in parts], axis=0),
                        jnp.concatenate([p[1] for p in parts], axis=0)))
        return tuple(out)

    init = tuple((jnp.full((TQ, 1), NEG, F32), jnp.zeros((TQ, 2 * HEAD_DIM), F32)) for _ in range(GROUP))
    res_s = lax.fori_loop(0, c_diag + 1, sel_step, init)

    gb = gb_ref[...]
    for r in range(GROUP):
        cols = slice(r * HEAD_DIM, (r + 1) * HEAD_DIM)
        g_s = gb[:, GROUP + r:GROUP + r + 1]
        g_w = gb[:, 2 * GROUP + r:2 * GROUP + r + 1]
        o_ref[:, cols] = oc_ref[:, cols] + g_s * _finish(res_s[r][1]) + g_w * o_w[r]


def _window_bias():
    t = np.arange(TQ)[None, :, None]
    k = np.arange(WIN_SPAN)[None, None, :] - np.arange(WINDOW // TQ + 1)[:, None, None] * TQ
    dist = t - k
    return jnp.asarray(np.where((dist >= 0) & (dist < WINDOW), 0.0, NEG), dtype=F32)


def _group_expand_matrix(n_keys, n_sel_blk):
    row = np.arange(LANES)[None, :, None]
    kk = np.arange(n_keys)[None, None, :]
    g = np.arange(N_KV)[:, None, None]
    return jnp.asarray(row == g * n_sel_blk + kk // SEL_BLOCK, dtype=BF16)


def _prompt_attn(q, sel, kvs_t, kvw_t, gb, o_c, slopes):
    batch, _, seq = kvs_t.shape
    t = batch * seq
    nt = seq // TQ
    assert seq % KS == 0 and seq >= WIN_SPAN
    expand = _group_expand_matrix(seq, -(-seq // SEL_BLOCK))
    wbias = _window_bias()
    gw = GROUP * HEAD_DIM
    tile = lambda n: pl.BlockSpec((TQ, n), lambda b, g, i: (b * nt + i, g))
    return pl.pallas_call(
        _prompt_attn_kernel,
        grid=(batch, N_KV, nt),
        in_specs=[pl.BlockSpec(memory_space=pltpu.SMEM),
                  tile(gw),
                  pl.BlockSpec((TQ, LANES), lambda b, g, i: (b * nt + i, 0)),
                  pl.BlockSpec((None, GKV, seq), lambda b, g, i: (b, g, 0)),
                  pl.BlockSpec((None, GKV, seq), lambda b, g, i: (b, g, 0)),
                  tile(LANES), tile(gw),
                  pl.BlockSpec((None, LANES, seq), lambda b, g, i: (g, 0, 0)),
                  pl.BlockSpec((None, TQ, WIN_SPAN), lambda b, g, i: (jnp.minimum(i, WINDOW // TQ), 0, 0))],
        out_specs=tile(gw),
        out_shape=jax.ShapeDtypeStruct((t, D_MODEL), F32),
        compiler_params=_params("parallel", "parallel", "arbitrary"),
        name="prompt_attn",
    )(slopes, q, sel, kvs_t, kvw_t, gb, o_c, expand, wbias)


def _stack_heads(q, g):
    return jnp.concatenate([q[:, (g * GROUP + r) * HEAD_DIM:(g * GROUP + r + 1) * HEAD_DIM]
                            for r in range(GROUP)], axis=0).astype(BF16)


def _row_t(t):
    return jnp.concatenate([lax.broadcasted_iota(jnp.int32, (t, 1), 0)] * GROUP, axis=0)


def _row_slopes(slopes_ref, g, t):
    row = lax.broadcasted_iota(jnp.int32, (GROUP * t, 1), 0)
    out = jnp.zeros((GROUP * t, 1), F32)
    for r in range(GROUP):
        out = jnp.where((row >= r * t) & (row < (r + 1) * t), slopes_ref[g * GROUP + r], out)
    return out


def _sample_cmp_kernel(slopes_ref, q_ref, p_ref, hc_ref, w2t_ref, b2_ref, ov_ref, oc_ref, sel_ref,
                       *, n_blk, past_len, blk_step):
    t = q_ref.shape[0]
    q = q_ref[...]
    n_pad = p_ref.shape[0]
    n_idx = lax.broadcasted_iota(jnp.int32, (1, n_pad), 1)
    end_rel = n_idx * CMP_STRIDE + (CMP_LEN - 1) - past_len
    valid = (_row_t(t) >= end_rel) & (n_idx < n_blk)
    end_rel_f = end_rel.astype(F32)
    p_sums = []
    for g in range(N_KV):
        kc, vc = _compress_finish(p_ref[:, g * 4 * CMP_HIDDEN:(g + 1) * 4 * CMP_HIDDEN], hc_ref, w2t_ref, b2_ref)
        s = _dot(_stack_heads(q, g), kc) + _row_slopes(slopes_ref, g, t) * end_rel_f
        p = _softmax_rows(s, valid)
        o = _dot_nt(p.astype(BF16), vc)
        p_sum = None
        for r in range(GROUP):
            h = g * GROUP + r
            oc_ref[:, h * HEAD_DIM:(h + 1) * HEAD_DIM] = o[r * t:(r + 1) * t, :]
            p_sum = p[0:t] if r == 0 else p_sum + p[r * t:(r + 1) * t]
        p_sums.append(p_sum)
    p_all = jnp.concatenate(p_sums, axis=0)
    ov = ov_ref[...]
    imp = sum(_dot(part, ov) for part in _split3(p_all))
    q_pos = jnp.concatenate([lax.broadcasted_iota(jnp.int32, (t, 1), 0)] * N_KV, axis=0) + past_len
    sel = _topk_mask(_block_scores(imp, q_pos, 1), N_SEL, 1)
    sel_ref[...] = jnp.zeros(sel_ref.shape, F32)
    for c in range(sel_ref.shape[0]):
        sel_ref[c, :, 0:blk_step] = sel[:, c * blk_step:(c + 1) * blk_step]


def _sample_cmp(q, pp, hc, w2t, b2, slopes, batch, past_len):
    t = q.shape[0] // batch
    n_ch = pp.shape[1]
    n_blk = n_ch - CMP_PARTS + 1
    n_sel_blk = -(-(past_len + t) // SEL_BLOCK)
    blk_step = PAGES_PER_STEP * PAGE_SIZE // SEL_BLOCK
    n_chunk = -(-n_sel_blk // blk_step)
    ov = _overlap_matrix(n_blk, n_ch, -(-n_chunk * blk_step // LANES) * LANES, False)
    return pl.pallas_call(
        functools.partial(_sample_cmp_kernel, n_blk=n_blk, past_len=past_len, blk_step=blk_step),
        grid=(batch,),
        in_specs=[pl.BlockSpec(memory_space=pltpu.SMEM),
                  pl.BlockSpec((t, D_MODEL), lambda b: (b, 0)),
                  pl.BlockSpec((None, n_ch, N_KV * 4 * CMP_HIDDEN), lambda b: (b, 0, 0)),
                  _full(hc.shape), _full(w2t.shape), _full(b2.shape), _full(ov.shape)],
        out_specs=[pl.BlockSpec((t, D_MODEL), lambda b: (b, 0)),
                   pl.BlockSpec((None, n_chunk, N_KV * t, LANES), lambda b: (b, 0, 0, 0))],
        out_shape=[jax.ShapeDtypeStruct((batch * t, D_MODEL), F32),
                   jax.ShapeDtypeStruct((batch, n_chunk, N_KV * t, LANES), F32)],
        compiler_params=_params("parallel"),
        name="sample_cmp",
    )(slopes, q, pp, hc, w2t, b2, ov)


def _sample_attn_kernel(*refs, past_len):
    n_pg = PAGES_PER_STEP
    pg_refs = refs[1:1 + n_pg]
    (slopes_ref, q_ref, selc_ref, sell_ref, ex_ref, kvs_ref, kvw_ref, win_ref, oc_ref, gb_ref,
     o_ref, nwin_ref, m_ref, acc_ref) = refs[1 + n_pg:]
    c = pl.program_id(1)
    n_step = pl.num_programs(1)
    t = q_ref.shape[0]
    rows = GROUP * t
    step_keys = n_pg * PAGE_SIZE
    row_t = _row_t(t)

    @pl.when(c == 0)
    def _():
        m_ref[...] = jnp.full(m_ref.shape, NEG, F32)
        acc_ref[...] = jnp.zeros(acc_ref.shape, F32)

    q = q_ref[...]
    k_rel = lax.broadcasted_iota(jnp.int32, (1, step_keys), 1) + (c * step_keys - past_len)
    k_rel_f = k_rel.astype(F32)
    ones = jnp.ones((HEAD_DIM, step_keys), BF16)
    expand = ex_ref[...]
    for g in range(N_KV):
        k_t = jnp.concatenate([pr[g, 0] for pr in pg_refs], axis=1).astype(BF16)
        v_aug_t = jnp.concatenate([jnp.concatenate([pr[g, 1] for pr in pg_refs], axis=1).astype(BF16), ones], axis=0)
        sel = jnp.concatenate([selc_ref[g * t:(g + 1) * t, :]] * GROUP, axis=0).astype(BF16)
        valid = (_dot(sel, expand) > 0.5) & (k_rel <= row_t)
        s = _dot(_stack_heads(q, g), k_t) + _row_slopes(slopes_ref, g, t) * k_rel_f
        m, acc = _online_update(jnp.where(valid, s, NEG), v_aug_t, m_ref[g], acc_ref[g])
        m_ref[g] = m
        acc_ref[g] = acc

    @pl.when(c == n_step - 1)
    def _():
        wb = win_ref.shape[-1]
        pad = jnp.zeros((LANES - t, KV_ROW), F32)
        new_s = jnp.concatenate([kvs_ref[...], pad], axis=0)
        new_w = jnp.concatenate([kvw_ref[...], pad], axis=0)
        ones_col = jnp.ones((LANES, HEAD_DIM), BF16)
        ones_old = jnp.ones((HEAD_DIM, wb), BF16)
        i_new = lax.broadcasted_iota(jnp.int32, (1, LANES), 1)
        i_old = lax.broadcasted_iota(jnp.int32, (1, wb), 1) - wb
        gb = gb_ref[...]

        def new_rows_update(new, lo, valid, slope, qs, m, acc):
            s = _dot_nt(qs, new[:, lo:lo + HEAD_DIM].astype(BF16)) + slope * i_new.astype(F32)
            s = jnp.where(valid, s, NEG)
            v_aug = jnp.concatenate([new[:, lo + HEAD_DIM:lo + GKV].astype(BF16), ones_col], axis=1)
            m_new = jnp.maximum(m, jnp.max(s, axis=-1, keepdims=True))
            p = jnp.exp2(s - m_new).astype(BF16)
            return m_new, jnp.exp2(m - m_new) * acc + _dot(p, v_aug)

        for g in range(N_KV):
            lo = g * GKV
            qs = _stack_heads(q, g)
            slope = _row_slopes(slopes_ref, g, t)
            sel_new = jnp.concatenate([sell_ref[g * t:(g + 1) * t, :]] * GROUP, axis=0)[:, 0:1]
            valid = (sel_new > 0.5) & (i_new <= row_t) & (i_new < t)
            _, acc = new_rows_update(new_s, lo, valid, slope, qs, m_ref[g], acc_ref[g])
            o_s = _finish(acc)
            dist = row_t - i_old
            valid = (dist >= 0) & (dist < WINDOW) & (i_old + past_len >= 0)
            s = _dot(qs, win_ref[g, 0].astype(BF16)) + slope * i_old.astype(F32)
            v_aug_t = jnp.concatenate([win_ref[g, 1].astype(BF16), ones_old], axis=0)
            m, acc = _online_update(jnp.where(valid, s, NEG), v_aug_t,
                                    jnp.full((rows, 1), NEG, F32), jnp.zeros((rows, 2 * HEAD_DIM), F32))
            dist = row_t - i_new
            valid = (dist >= 0) & (dist < WINDOW) & (i_new < t)
            _, acc = new_rows_update(new_w, lo, valid, slope, qs, m, acc)
            o_w = _finish(acc)
            for r in range(GROUP):
                h = g * GROUP + r
                gate = [gb[:, g * LANES + br * GROUP + r:g * LANES + br * GROUP + r + 1] for br in range(3)]
                o_ref[:, h * HEAD_DIM:(h + 1) * HEAD_DIM] = (
                    gate[0] * oc_ref[:, h * HEAD_DIM:(h + 1) * HEAD_DIM]
                    + gate[1] * o_s[r * t:(r + 1) * t, :] + gate[2] * o_w[r * t:(r + 1) * t, :])
        new_t = pltpu.roll(new_w.T, LANES - t, 1)
        lane = lax.broadcasted_iota(jnp.int32, (KV_ROW, LANES), 1)
        old = pltpu.roll(win_ref[...].reshape(KV_ROW, wb), wb - t, 1)
        nwin_ref[:, 0:wb - LANES] = old[:, 0:wb - LANES]
        nwin_ref[:, wb - LANES:wb] = jnp.where(lane < LANES - t, old[:, wb - LANES:wb], new_t)


def _sample_attn(q, selc, kvs, kvw, cache_t, win_t, page_table, o_c, gb, slopes, layer, past_len):
    batch, n_pages = page_table.shape
    t = q.shape[0] // batch
    n_step = n_pages // PAGES_PER_STEP
    wb = win_t.shape[-1]
    expand = _block_expand_matrix(PAGES_PER_STEP * PAGE_SIZE)
    n_chunk = selc.shape[1]
    tok = lambda n: pl.BlockSpec((t, n), lambda b, c, pt: (b, 0))
    return pl.pallas_call(
        functools.partial(_sample_attn_kernel, past_len=past_len),
        grid_spec=pltpu.PrefetchScalarGridSpec(
            num_scalar_prefetch=1,
            grid=(batch, n_step),
            in_specs=[_page_spec(layer, k) for k in range(PAGES_PER_STEP)] + [
                pl.BlockSpec(memory_space=pltpu.SMEM),
                tok(D_MODEL),
                pl.BlockSpec((None, None, N_KV * t, LANES), lambda b, c, pt: (b, c, 0, 0)),
                pl.BlockSpec((None, None, N_KV * t, LANES), lambda b, c, pt: (b, n_chunk - 1, 0, 0)),
                pl.BlockSpec(expand.shape, lambda b, c, pt: (0, 0)),
                tok(KV_ROW), tok(KV_ROW),
                pl.BlockSpec((None, None, N_KV, 2, HEAD_DIM, wb), lambda b, c, pt: (layer, b, 0, 0, 0, 0)),
                tok(D_MODEL), tok(N_KV * LANES)],
            out_specs=[tok(D_MODEL), pl.BlockSpec((None, KV_ROW, wb), lambda b, c, pt: (b, 0, 0))],
            scratch_shapes=[pltpu.VMEM((N_KV, GROUP * t, 1), F32),
                            pltpu.VMEM((N_KV, GROUP * t, 2 * HEAD_DIM), F32)],
        ),
        out_shape=[jax.ShapeDtypeStruct((batch * t, D_MODEL), F32),
                   jax.ShapeDtypeStruct((batch, KV_ROW, wb), F32)],
        compiler_params=_params("parallel", "arbitrary"),
        name="sample_attn",
    )(page_table, *([cache_t] * PAGES_PER_STEP), slopes, q, selc, selc, expand, kvs, kvw, win_t, o_c, gb)


HIST = 32


def _conv_kernel(z_ref, prev_ref, hist_ref, w_ref, b_ref, y_ref, st_ref, zc_ref, zs_ref):
    i = pl.program_id(1)
    tm = z_ref.shape[0]

    @pl.when(i == 0)
    def _():
        zc_ref[0:HIST, :] = hist_ref[...]

    @pl.when(i > 0)
    def _():
        zc_ref[0:HIST, :] = prev_ref[...]

    zc_ref[HIST:HIST + tm, :] = z_ref[...]
    span = HIST + tm - SUBLANES
    for rho in range(1, SUBLANES):
        zs_ref[rho, 0:span, :] = zc_ref[rho:rho + span, :]
    off = HIST - (CONV_W - 1)
    for cb in range(C_CONV // LANES):
        cols = slice(cb * LANES, (cb + 1) * LANES)
        acc = jnp.zeros((tm, LANES), F32) + b_ref[:, cols]
        for k in range(CONV_W):
            rho = (off + k) % SUBLANES
            base = off + k - rho
            tap = zc_ref[base:base + tm, cols] if rho == 0 else zs_ref[rho, base:base + tm, cols]
            acc = acc + tap * w_ref[k:k + 1, cols]
        y_ref[:, cols] = acc

    @pl.when(i == pl.num_programs(1) - 1)
    def _():
        st_ref[...] = zc_ref[HIST + tm - (CONV_W - 1):HIST + tm, :]


def _conv(z, hist, w, b, batch, tm):
    seq = z.shape[0] // batch
    nt = seq // tm
    per = tm // HIST if tm >= HIST else 1
    z3 = z.reshape(batch, seq, C_CONV)
    if tm >= HIST:
        prev_spec = pl.BlockSpec((None, HIST, C_CONV), lambda b_, i: (b_, jnp.maximum(i * per - 1, 0), 0))
        prev = z3
    else:
        prev_spec = pl.BlockSpec((None, HIST, C_CONV), lambda b_, i: (b_, 0, 0))
        prev = hist
    y, st = pl.pallas_call(
        _conv_kernel,
        grid=(batch, nt),
        in_specs=[pl.BlockSpec((None, tm, C_CONV), lambda b_, i: (b_, i, 0)),
                  prev_spec,
                  pl.BlockSpec((None, HIST, C_CONV), lambda b_, i: (b_, 0, 0)),
                  _full(w.shape), _full(b.shape)],
        out_specs=[pl.BlockSpec((None, tm, C_CONV), lambda b_, i: (b_, i, 0)),
                   pl.BlockSpec((None, CONV_W - 1, C_CONV), lambda b_, i: (b_, 0, 0))],
        out_shape=[jax.ShapeDtypeStruct((batch, seq, C_CONV), F32),
                   jax.ShapeDtypeStruct((batch, CONV_W - 1, C_CONV), F32)],
        scratch_shapes=[pltpu.VMEM((HIST + tm, C_CONV), F32), pltpu.VMEM((SUBLANES, HIST + tm, C_CONV), F32)],
        compiler_params=_params("parallel", "arbitrary"),
        name="conv",
    )(z3, prev, hist, w, b)
    return y.reshape(batch * seq, C_CONV), st


def _post_kernel(x_ref, y_ref, oa_ref, gm_ref, ple_ref,
                 wpw_ref, wout_ref, wg_ref, wu_ref, wd_ref, wpg_ref, wpp_ref,
                 cg_ref, cb_ref, bpw_ref, g1_ref, b1_ref, g2_ref, b2_ref, o_ref, *, alpha):
    yn = _layernorm(y_ref[...], cg_ref[...], cb_ref[...])
    o_b = _dot((yn * _sigmoid(yn)).astype(BF16), wpw_ref[...]) + bpw_ref[...]
    mixed = gm_ref[:, 0:D_MODEL] * oa_ref[...] + gm_ref[:, D_MODEL:2 * D_MODEL] * o_b
    x1 = _layernorm(alpha * x_ref[...] + _dot(mixed.astype(BF16), wout_ref[...]), g1_ref[...], b1_ref[...])
    x1b = x1.astype(BF16)
    hg = _dot(x1b, wg_ref[...])
    hu = _dot(x1b, wu_ref[...])
    f = _dot((hg * _sigmoid(hg) * hu).astype(BF16), wd_ref[...])
    x2 = _layernorm(alpha * x1 + f, g2_ref[...], b2_ref[...])
    gate = _sigmoid(_dot(x2.astype(BF16), wpg_ref[...]))
    o_ref[...] = x2 + gate * _dot(ple_ref[...].astype(BF16), wpp_ref[...])


def _post(x, y, o_a, gm, ple, w, alpha, tm):
    t = x.shape[0]
    row = lambda n: pl.BlockSpec((tm, n), lambda i: (i, 0))
    mats = [w["w_pw2"], w["w_out"], w["w_ffn_gate"], w["w_ffn_up"], w["w_ffn_down"], w["w_ple_gate"], w["w_ple_in"]]
    vecs = [w["conv_ln_g"], w["conv_ln_b"], w["b_pw2"], w["ln1_g"], w["ln1_b"], w["ln2_g"], w["ln2_b"]]
    return pl.pallas_call(
        functools.partial(_post_kernel, alpha=alpha),
        grid=(t // tm,),
        in_specs=[row(D_MODEL), row(C_CONV), row(D_MODEL), row(2 * D_MODEL), row(ple.shape[1])]
                 + [_resident(m.shape) for m in mats] + [_resident(v.shape) for v in vecs],
        out_specs=row(D_MODEL),
        out_shape=jax.ShapeDtypeStruct((t, D_MODEL), F32),
        compiler_params=_params("parallel"),
        name="post",
    )(x, y, o_a, gm, ple, *mats, *vecs)


def _prep_layer(l, w_in, b_in, cmp_w1, cmp_w2, cmp_b2, named):
    offs = np.cumsum((D_MODEL, 3 * KV_ROW, 3 * N_HEADS, 2 * C_CONV, 2 * D_MODEL))
    wl, bl = w_in[l], b_in[l]
    qs = ATTN_SCALE * LOG2E
    new_cols = np.array([g * LANES + br * GROUP + r for br in range(3) for g in range(N_KV) for r in range(GROUP)])
    wgb = jnp.zeros((D_MODEL, N_KV * LANES), F32).at[:, new_cols].set(wl[:, offs[1]:offs[2]])
    bgb = jnp.zeros((N_KV * LANES,), F32).at[new_cols].set(bl[offs[1]:offs[2]])
    out = {
        "wq": (wl[:, :offs[0]] * qs).astype(BF16), "bq": (bl[:offs[0]] * qs)[None],
        "wkvT": wl[:, offs[0]:offs[1]].T.astype(BF16),
        "bkv_row": bl[offs[0]:offs[1]][None], "bkv_col": bl[offs[0]:offs[1]][:, None],
        "wgb": wgb.astype(BF16), "bgb": bgb[None],
        "wu": wl[:, offs[2]:offs[3]].astype(BF16), "bu": bl[offs[2]:offs[3]][None],
        "wgm": wl[:, offs[3]:offs[4]].astype(BF16), "bgm": bl[offs[3]:offs[4]][None],
    }
    w1 = cmp_w1[l].reshape(2, CMP_PARTS, CMP_STRIDE, HEAD_DIM, CMP_HIDDEN).transpose(2, 0, 3, 1, 4)
    zero = jnp.zeros_like(w1[:, 0])
    w1cat = jnp.stack([jnp.stack([w1[:, 0], zero], axis=2), jnp.stack([zero, w1[:, 1]], axis=2)], axis=1)
    out["w1cat"] = w1cat.reshape(CMP_STRIDE * 2 * HEAD_DIM, 4 * CMP_HIDDEN).astype(BF16)
    out["w2t"] = jnp.transpose(cmp_w2[l], (0, 2, 1)).astype(BF16)
    out["b2"] = cmp_b2[l][:, :, None]
    for name, val in named.items():
        v = val[l]
        out[name] = v.astype(BF16) if v.ndim == 2 else v[None]
    return out


def _kv_out(stacked_t):
    l, b, _, t = stacked_t.shape
    return jnp.transpose(stacked_t.reshape(l, b, N_KV, 2, HEAD_DIM, t), (0, 1, 5, 2, 3, 4))


def kernel(x_prompt, x_sample, cache_cmp, cache_sel, state_win, state_conv, page_table, p_prompt, p_sample,
           w_in, b_in, cmp_w1, cmp_pe, cmp_b1, cmp_w2, cmp_b2, conv_w, conv_b, conv_ln_g, conv_ln_b,
           w_pw2, b_pw2, w_out, ln1_g, ln1_b, w_ffn_gate, w_ffn_up, w_ffn_down, ln2_g, ln2_b,
           w_ple_in, w_ple_gate):
    depth = w_in.shape[0]
    alpha = float((2 * depth) ** 0.25)
    batch, seq = x_prompt.shape[:2]
    dbatch, dseq = x_sample.shape[:2]
    past_len = page_table.shape[1] * PAGE_SIZE
    slopes = jnp.exp2(-8.0 * jnp.arange(1, N_HEADS + 1, dtype=F32) / N_HEADS) * LOG2E
    hconst = _pe_term(cmp_pe, cmp_w1, cmp_b1)
    named = dict(w_pw2=w_pw2, w_out=w_out, w_ffn_gate=w_ffn_gate, w_ffn_up=w_ffn_up, w_ffn_down=w_ffn_down,
                 w_ple_gate=w_ple_gate, w_ple_in=w_ple_in, conv_ln_g=conv_ln_g, conv_ln_b=conv_ln_b,
                 b_pw2=b_pw2, ln1_g=ln1_g, ln1_b=ln1_b, ln2_g=ln2_g, ln2_b=ln2_b, conv_b=conv_b)
    layers = [_prep_layer(l, w_in, b_in, cmp_w1, cmp_w2, cmp_b2, named) for l in range(depth)]

    x = x_prompt.reshape(batch * seq, D_MODEL)
    zero_hist = jnp.zeros((batch, HIST, C_CONV), F32)
    out_p = [[], [], [], []]
    for l, w in enumerate(layers):
        q, kvc_t, kvs_t, kvw_t, gb, z, gm = _in_proj(x, w, batch, BF16, 256, True)
        pp = _cmp_partial_prompt(kvc_t, w["w1cat"])
        o_c, sel = _prompt_select(q, pp, gb, hconst[l], w["w2t"], w["b2"], slopes, batch)
        o_a = _prompt_attn(q, sel, kvs_t, kvw_t, gb, o_c, slopes)
        y, st = _conv(z, zero_hist, conv_w[l], w["conv_b"], batch, 256)
        x = _post(x, y, o_a, gm, p_prompt[l].reshape(batch * seq, -1), w, alpha, 256)
        out_p[0].append(kvc_t)
        out_p[1].append(kvs_t)
        out_p[2].append(kvw_t[:, :, seq - min(WINDOW, seq):])
        out_p[3].append(st)
    y_prompt = x.reshape(batch, seq, D_MODEL)

    x = x_sample.reshape(dbatch * dseq, D_MODEL)
    cmp_t = _paged_view(cache_cmp)
    sel_t = _paged_view(cache_sel)
    win_t = jnp.transpose(state_win, (0, 1, 3, 4, 5, 2))
    out_s = [[], [], [], []]
    for l, w in enumerate(layers):
        q, kvc, kvs, kvw, gb, z, gm = _in_proj(x, w, 1, F32, dbatch * dseq, False)
        pp = _cmp_partial_sample(cmp_t, page_table, w["w1cat"], l)
        o_c, selc = _sample_cmp(q, pp, hconst[l], w["w2t"], w["b2"], slopes, dbatch, past_len)
        o_a, nwin = _sample_attn(q, selc, kvs, kvw, sel_t, win_t, page_table, o_c, gb, slopes, l, past_len)
        hist = jnp.pad(state_conv[l], ((0, 0), (HIST - (CONV_W - 1), 0), (0, 0)))
        y, st = _conv(z, hist, conv_w[l], w["conv_b"], dbatch, dseq)
        x = _post(x, y, o_a, gm, p_sample[l].reshape(dbatch * dseq, -1), w, alpha, dbatch * dseq)
        out_s[0].append(kvc.reshape(dbatch, dseq, N_KV, 2, HEAD_DIM))
        out_s[1].append(kvs.reshape(dbatch, dseq, N_KV, 2, HEAD_DIM))
        out_s[2].append(nwin)
        out_s[3].append(st)
    y_sample = x.reshape(dbatch, dseq, D_MODEL)

    return (y_prompt, y_sample, _kv_out(jnp.stack(out_p[0])), _kv_out(jnp.stack(out_p[1])),
            _kv_out(jnp.stack(out_p[2])), jnp.stack(out_p[3]),
            jnp.stack(out_s[0]), jnp.stack(out_s[1]), _kv_out(jnp.stack(out_s[2])), jnp.stack(out_s[3]))
```

```python
import functools
import math

import numpy as np
import jax
import jax.numpy as jnp
from jax import lax
from jax.experimental import pallas as pl
from jax.experimental.pallas import tpu as pltpu

F32 = jnp.float32
BF16 = jnp.bfloat16

D_MODEL = 1024
N_HEADS = 16
HEAD_DIM = 64
N_KV = 4
GROUP = N_HEADS // N_KV
KV_ROW = N_KV * 2 * HEAD_DIM
GKV = 2 * HEAD_DIM
CMP_LEN = 32
CMP_STRIDE = 16
CMP_PARTS = CMP_LEN // CMP_STRIDE
CMP_HIDDEN = 2 * HEAD_DIM
SEL_BLOCK = 64
SEL_SHIFT = SEL_BLOCK.bit_length() - 1
N_SEL = 16
WINDOW = 512
C_CONV = D_MODEL
CONV_W = 31
PAGE_SIZE = 128
CH_PAGE = PAGE_SIZE // CMP_STRIDE
LN_EPS = 1e-5
ATTN_SCALE = HEAD_DIM ** -0.5
LOG2E = math.log2(math.e)
FORCE_SCORE = 1e4
NEG = -1e30
LANES = 128
SUBLANES = 8
VMEM_LIMIT = 56 * 1024 * 1024

PAGES_PER_STEP = 16
SEL_PAGES_PER_STEP = 32
TQ = 256
KS = 512


def _dot(a, b):
    return jnp.dot(a, b, preferred_element_type=F32)


def _dot_nt(a, b):
    return lax.dot_general(a, b, (((1,), (1,)), ((), ())), preferred_element_type=F32)


def _split3(a):
    a1 = a.astype(BF16)
    r1 = a - a1.astype(F32)
    a2 = r1.astype(BF16)
    return a1, a2, (r1 - a2.astype(F32)).astype(BF16)


def _sigmoid(x):
    return 1.0 / (1.0 + jnp.exp(-x))


def _layernorm(x, g, b):
    mu = jnp.mean(x, axis=-1, keepdims=True)
    xc = x - mu
    var = jnp.mean(xc * xc, axis=-1, keepdims=True)
    return xc * lax.rsqrt(var + LN_EPS) * g + b


def _params(*sem):
    return pltpu.CompilerParams(dimension_semantics=sem, vmem_limit_bytes=VMEM_LIMIT)


def _full(shape):
    n = len(shape)
    return pl.BlockSpec(shape, lambda *_: (0,) * n)


def _resident(shape):
    n = len(shape)
    return pl.BlockSpec(shape, lambda *_: (0,) * n, pipeline_mode=pl.Buffered(1))


def _pe_term_kernel(pe_ref, w1_ref, b1_ref, o_ref):
    o_ref[...] = b1_ref[...] + jnp.sum(pe_ref[...] * w1_ref[...], axis=0, keepdims=True)


def _pe_term(cmp_pe, cmp_w1, cmp_b1):
    n = cmp_pe.shape[0] * 2
    k = CMP_LEN * HEAD_DIM
    out = pl.pallas_call(
        _pe_term_kernel,
        grid=(n,),
        in_specs=[pl.BlockSpec((None, k, 1), lambda i: (i, 0, 0)),
                  pl.BlockSpec((None, k, CMP_HIDDEN), lambda i: (i, 0, 0)),
                  pl.BlockSpec((None, 1, CMP_HIDDEN), lambda i: (i, 0, 0))],
        out_specs=pl.BlockSpec((None, 1, CMP_HIDDEN), lambda i: (i, 0, 0)),
        out_shape=jax.ShapeDtypeStruct((n, 1, CMP_HIDDEN), F32),
        compiler_params=_params("parallel"),
        name="pe_term",
    )(cmp_pe.reshape(n, k, 1), cmp_w1.reshape(n, k, CMP_HIDDEN), cmp_b1.reshape(n, 1, CMP_HIDDEN))
    return out.reshape(-1, 2, CMP_HIDDEN)


def _in_proj_kernel(x_ref, wq_ref, wkv_ref, wgb_ref, wu_ref, wgm_ref,
                    bq_ref, bkv_ref, bgb_ref, bu_ref, bgm_ref,
                    q_ref, kvc_ref, kvs_ref, kvw_ref, gb_ref, z_ref, gm_ref, *, kv_t):
    xb = x_ref[...].astype(BF16)
    q_ref[...] = (_dot(xb, wq_ref[...]) + bq_ref[...]).astype(q_ref.dtype)
    if kv_t:
        kv = _dot_nt(wkv_ref[...], xb) + bkv_ref[...]
        kvc_ref[...] = kv[0:KV_ROW, :]
        kvs_ref[...] = kv[KV_ROW:2 * KV_ROW, :]
        kvw_ref[...] = kv[2 * KV_ROW:3 * KV_ROW, :]
    else:
        kv = _dot_nt(xb, wkv_ref[...]) + bkv_ref[...]
        kvc_ref[...] = kv[:, 0:KV_ROW]
        kvs_ref[...] = kv[:, KV_ROW:2 * KV_ROW]
        kvw_ref[...] = kv[:, 2 * KV_ROW:3 * KV_ROW]
    gb_ref[...] = _sigmoid(_dot(xb, wgb_ref[...]) + bgb_ref[...])
    u = _dot(xb, wu_ref[...]) + bu_ref[...]
    z_ref[...] = u[:, 0:C_CONV] * _sigmoid(u[:, C_CONV:2 * C_CONV])
    gm_ref[...] = _sigmoid(_dot(xb, wgm_ref[...]) + bgm_ref[...])


def _in_proj(x, w, batch, q_dtype, tm, kv_t):
    t = x.shape[0]
    nt = t // batch // tm
    row = lambda n: pl.BlockSpec((tm, n), lambda b, i: (b * nt + i, 0))
    mats = [w["wq"], w["wkvT"], w["wgb"], w["wu"], w["wgm"]]
    vecs = [w["bq"], w["bkv_col"] if kv_t else w["bkv_row"], w["bgb"], w["bu"], w["bgm"]]
    if kv_t:
        kv_spec = pl.BlockSpec((None, KV_ROW, tm), lambda b, i: (b, 0, i))
        kv_shape = jax.ShapeDtypeStruct((batch, KV_ROW, t // batch), F32)
    else:
        kv_spec = row(KV_ROW)
        kv_shape = jax.ShapeDtypeStruct((t, KV_ROW), F32)
    return pl.pallas_call(
        functools.partial(_in_proj_kernel, kv_t=kv_t),
        grid=(batch, nt),
        in_specs=[row(D_MODEL)] + [_resident(m.shape) for m in mats] + [_resident(v.shape) for v in vecs],
        out_specs=[row(D_MODEL), kv_spec, kv_spec, kv_spec, row(N_KV * LANES), row(C_CONV), row(2 * D_MODEL)],
        out_shape=[jax.ShapeDtypeStruct((t, D_MODEL), q_dtype), kv_shape, kv_shape, kv_shape,
                   jax.ShapeDtypeStruct((t, N_KV * LANES), F32),
                   jax.ShapeDtypeStruct((t, C_CONV), F32),
                   jax.ShapeDtypeStruct((t, 2 * D_MODEL), F32)],
        compiler_params=_params("parallel", "parallel"),
        name="in_proj",
    )(x, *mats, *vecs)


def _cmp_partial_body(page_fn, n_pages, perm_ref, w_ref, p_ref):
    perm = perm_ref[...]
    regrouped = [[_dot_nt(perm, page_fn(k, g).astype(BF16)) for k in range(n_pages)] for g in range(N_KV)]
    a = jnp.concatenate(
        [jnp.concatenate([regrouped[g][k][j * CH_PAGE:(j + 1) * CH_PAGE, :]
                          for g in range(N_KV) for k in range(n_pages)], axis=0).astype(BF16)
         for j in range(CMP_STRIDE)], axis=1)
    acc = _dot(a, w_ref[...])
    rows = n_pages * CH_PAGE
    for g in range(N_KV):
        p_ref[:, g * 4 * CMP_HIDDEN:(g + 1) * 4 * CMP_HIDDEN] = acc[g * rows:(g + 1) * rows, :]


def _cmp_partial_prompt_kernel(x_ref, perm_ref, w_ref, p_ref):
    n_pages = x_ref.shape[1] // PAGE_SIZE
    page_fn = lambda k, g: x_ref[g * GKV:(g + 1) * GKV, k * PAGE_SIZE:(k + 1) * PAGE_SIZE]
    _cmp_partial_body(page_fn, n_pages, perm_ref, w_ref, p_ref)


def _cmp_partial_sample_kernel(*refs):
    pg_refs = refs[1:1 + PAGES_PER_STEP]
    perm_ref, w_ref, p_ref = refs[1 + PAGES_PER_STEP:]
    page_fn = lambda k, g: pg_refs[k][g].reshape(GKV, PAGE_SIZE)
    _cmp_partial_body(page_fn, PAGES_PER_STEP, perm_ref, w_ref, p_ref)


def _row_regroup_matrix():
    out = np.zeros((PAGE_SIZE, PAGE_SIZE), np.float32)
    for j in range(CMP_STRIDE):
        for c in range(CH_PAGE):
            out[j * CH_PAGE + c, c * CMP_STRIDE + j] = 1.0
    return jnp.asarray(out, dtype=BF16)


def _cmp_partial_prompt(kvc_t, w1cat):
    batch, _, seq = kvc_t.shape
    n_ch = seq // CMP_STRIDE
    perm = _row_regroup_matrix()
    return pl.pallas_call(
        _cmp_partial_prompt_kernel,
        grid=(batch,),
        in_specs=[pl.BlockSpec((None, KV_ROW, seq), lambda b: (b, 0, 0)),
                  _resident(perm.shape), _resident(w1cat.shape)],
        out_specs=pl.BlockSpec((None, n_ch, N_KV * 4 * CMP_HIDDEN), lambda b: (b, 0, 0)),
        out_shape=jax.ShapeDtypeStruct((batch, n_ch, N_KV * 4 * CMP_HIDDEN), F32),
        compiler_params=_params("parallel"),
        name="cmp_partial_prompt",
    )(kvc_t, perm, w1cat)


def _paged_view(cache):
    return jnp.transpose(cache, (0, 1, 3, 4, 5, 2))


def _page_spec(layer, k, per_step):
    return pl.BlockSpec((None, None, N_KV, 2, HEAD_DIM, PAGE_SIZE),
                        lambda b, c, pt: (layer, pt[b, c * per_step + k], 0, 0, 0, 0))


def _cmp_partial_sample(cache_t, page_table, w1cat, layer):
    batch, n_pages = page_table.shape
    n_step = n_pages // PAGES_PER_STEP
    rows = PAGES_PER_STEP * CH_PAGE
    perm = _row_regroup_matrix()
    const = lambda shape: pl.BlockSpec(shape, lambda b, c, pt: (0,) * len(shape), pipeline_mode=pl.Buffered(1))
    return pl.pallas_call(
        _cmp_partial_sample_kernel,
        grid_spec=pltpu.PrefetchScalarGridSpec(
            num_scalar_prefetch=1,
            grid=(batch, n_step),
            in_specs=[_page_spec(layer, k, PAGES_PER_STEP) for k in range(PAGES_PER_STEP)]
                     + [const(perm.shape), const(w1cat.shape)],
            out_specs=pl.BlockSpec((None, rows, N_KV * 4 * CMP_HIDDEN), lambda b, c, pt: (b, c, 0)),
        ),
        out_shape=jax.ShapeDtypeStruct((batch, n_step * rows, N_KV * 4 * CMP_HIDDEN), F32),
        compiler_params=_params("parallel", "parallel"),
        name="cmp_partial_sample",
    )(page_table, *([cache_t] * PAGES_PER_STEP), perm, w1cat)


def _compress_finish(pb, hc_ref, w2t_ref, b2_ref):
    n = pb.shape[0]
    out = []
    for s in range(2):
        p0 = pb[:, s * 2 * CMP_HIDDEN:s * 2 * CMP_HIDDEN + CMP_HIDDEN]
        p1 = pb[:, s * 2 * CMP_HIDDEN + CMP_HIDDEN:(s + 1) * 2 * CMP_HIDDEN]
        h = hc_ref[s:s + 1, :] + p0 + pltpu.roll(p1, n - 1, 0)
        a = h * _sigmoid(h)
        out.append((_dot_nt(w2t_ref[s], a.astype(BF16)) + b2_ref[s]).astype(BF16))
    return out


def _softmax_rows(s, valid):
    s = jnp.where(valid, s, NEG)
    m = jnp.max(s, axis=-1, keepdims=True)
    e = jnp.where(valid, jnp.exp2(s - m), 0.0)
    return e * (1.0 / jnp.maximum(jnp.sum(e, axis=-1, keepdims=True), 1e-30))


def _topk_mask(score, n_top, axis):
    idx = lax.broadcasted_iota(jnp.int32, score.shape, axis).astype(F32)
    big = float(score.shape[axis])
    sel = jnp.zeros(score.shape, F32)
    for _ in range(n_top):
        m = jnp.max(score, axis=axis, keepdims=True)
        first = jnp.min(jnp.where(score == m, idx, big), axis=axis, keepdims=True)
        hit = idx == first
        sel = jnp.where(hit & (m > -0.5), 1.0, sel)
        score = jnp.where(hit, -3e38, score)
    return sel


def _block_scores(imp, q_pos, axis):
    j = lax.broadcasted_iota(jnp.int32, imp.shape, axis)
    cur = jnp.right_shift(q_pos, SEL_SHIFT)
    forced = (j == 0) | (j == cur) | (j == cur - 1)
    return jnp.where(j > cur, -1.0, jnp.where(forced, FORCE_SCORE, imp))


def _online_update(s, v_aug_t, m, acc):
    m_new = jnp.maximum(m, jnp.max(s, axis=-1, keepdims=True))
    p = jnp.exp2(s - m_new).astype(BF16)
    acc = jnp.exp2(m - m_new) * acc + _dot_nt(p, v_aug_t)
    return m_new, acc


def _finish(acc):
    return acc[:, 0:HEAD_DIM] * (1.0 / jnp.maximum(acc[:, HEAD_DIM:2 * HEAD_DIM], 1e-30))


def _overlap_matrix(n_blk, n_blk_pad, n_sel_pad, transpose):
    n = np.arange(n_blk_pad)[:, None]
    j = np.arange(n_sel_pad)[None, :]
    ov = (n * CMP_STRIDE < j * SEL_BLOCK + SEL_BLOCK) & (n * CMP_STRIDE + CMP_LEN > j * SEL_BLOCK) & (n < n_blk)
    return jnp.asarray(ov.T if transpose else ov, dtype=BF16)


def _block_expand_matrix(n_keys):
    jj = np.arange(LANES)[:, None]
    kk = np.arange(n_keys)[None, :]
    return jnp.asarray(jj == kk // SEL_BLOCK, dtype=BF16)


def _prompt_select_kernel(slopes_ref, q_ref, p_ref, gb_ref, hc_ref, w2t_ref, b2_ref, ovt_ref,
                          oc_ref, sel_ref, kc_ref, vc_ref, *, n_blk, n_sel_blk):
    i = pl.program_id(1)
    t0 = i * TQ

    @pl.when(i == 0)
    def _():
        for g in range(N_KV):
            kc, vc = _compress_finish(p_ref[:, g * 4 * CMP_HIDDEN:(g + 1) * 4 * CMP_HIDDEN],
                                      hc_ref, w2t_ref, b2_ref)
            kc_ref[g] = kc
            vc_ref[g] = vc

    q = q_ref[...]
    gb = gb_ref[...]
    t_rel = lax.broadcasted_iota(jnp.int32, (TQ, 1), 0)
    n_pad = kc_ref.shape[2]
    n_idx = lax.broadcasted_iota(jnp.int32, (1, n_pad), 1)
    end_rel = n_idx * CMP_STRIDE + (CMP_LEN - 1) - t0
    valid_c = (t_rel >= end_rel) & (n_idx < n_blk)
    end_rel_f = end_rel.astype(F32)
    ovt = ovt_ref[...]
    imp_parts = []
    for g in range(N_KV):
        kc = kc_ref[g]
        vc = vc_ref[g]
        p_sum = None
        for r in range(GROUP):
            h = g * GROUP + r
            s = _dot(q[:, h * HEAD_DIM:(h + 1) * HEAD_DIM], kc) + slopes_ref[h] * end_rel_f
            p = _softmax_rows(s, valid_c)
            gate = gb[:, g * LANES + r:g * LANES + r + 1]
            oc_ref[:, h * HEAD_DIM:(h + 1) * HEAD_DIM] = gate * _dot_nt(p.astype(BF16), vc)
            p_sum = p if p_sum is None else p_sum + p
        imp_parts.append(sum(_dot_nt(ovt, part) for part in _split3(p_sum))[0:n_sel_blk, :])
    imp_t = jnp.stack(imp_parts, axis=0)
    q_pos = lax.broadcasted_iota(jnp.int32, (1, 1, TQ), 2) + t0
    sel_t = _topk_mask(_block_scores(imp_t, q_pos, 1), N_SEL, 1).reshape(N_KV * n_sel_blk, TQ)
    sel_ref[...] = sel_t.T.astype(BF16)


def _prompt_select(q, pp, gb, hc, w2t, b2, slopes, batch):
    t = q.shape[0]
    seq = t // batch
    nt = seq // TQ
    n_ch = pp.shape[1]
    n_blk = n_ch - CMP_PARTS + 1
    n_sel_blk = -(-seq // SEL_BLOCK)
    assert N_KV * n_sel_blk == LANES
    ovt = _overlap_matrix(n_blk, n_ch, LANES, True)
    row = lambda n: pl.BlockSpec((TQ, n), lambda b, i: (b * nt + i, 0))
    return pl.pallas_call(
        functools.partial(_prompt_select_kernel, n_blk=n_blk, n_sel_blk=n_sel_blk),
        grid=(batch, nt),
        in_specs=[pl.BlockSpec(memory_space=pltpu.SMEM),
                  row(D_MODEL),
                  pl.BlockSpec((None, n_ch, N_KV * 4 * CMP_HIDDEN), lambda b, i: (b, 0, 0)),
                  row(N_KV * LANES),
                  _full(hc.shape), _full(w2t.shape), _full(b2.shape), _full(ovt.shape)],
        out_specs=[row(D_MODEL), row(LANES)],
        out_shape=[jax.ShapeDtypeStruct((t, D_MODEL), F32), jax.ShapeDtypeStruct((t, LANES), BF16)],
        scratch_shapes=[pltpu.VMEM((N_KV, HEAD_DIM, n_ch), BF16), pltpu.VMEM((N_KV, HEAD_DIM, n_ch), BF16)],
        compiler_params=_params("parallel", "arbitrary"),
        name="prompt_select",
    )(slopes, q, pp, gb, hc, w2t, b2, ovt)


WIN_SPAN = WINDOW + TQ


def _prompt_attn_kernel(slopes_ref, q_ref, sel_ref, kvs_ref, kvw_ref, gb_ref, oc_ref, ex_ref, wbias_ref, o_ref):
    g = pl.program_id(1)
    i = pl.program_id(2)
    t0 = i * TQ
    q = q_ref[...]
    q_all = jnp.concatenate([q[:, r * HEAD_DIM:(r + 1) * HEAD_DIM] for r in range(GROUP)], axis=0)
    slopes = [slopes_ref[g * GROUP + r] for r in range(GROUP)]
    t_rel = lax.broadcasted_iota(jnp.int32, (TQ, 1), 0)

    def scores(k_t, k_rel_f, bias):
        s = _dot(q_all, k_t)
        return jnp.concatenate([s[r * TQ:(r + 1) * TQ] + (slopes[r] * k_rel_f + bias) for r in range(GROUP)], axis=0)

    start = pl.multiple_of(jnp.maximum(t0 - WINDOW, 0), TQ)
    k_t = kvw_ref[0:HEAD_DIM, pl.ds(start, WIN_SPAN)].astype(BF16)
    v_aug_t = jnp.concatenate([kvw_ref[HEAD_DIM:GKV, pl.ds(start, WIN_SPAN)].astype(BF16),
                               jnp.ones((HEAD_DIM, WIN_SPAN), BF16)], axis=0)
    k_rel_f = (lax.broadcasted_iota(jnp.int32, (1, WIN_SPAN), 1) + (start - t0)).astype(F32)
    s = scores(k_t, k_rel_f, wbias_ref[...])
    p = jnp.exp2(s - jnp.max(s, axis=-1, keepdims=True)).astype(BF16)
    o_w = _finish(_dot_nt(p, v_aug_t))

    sel = sel_ref[...]
    ones = jnp.ones((HEAD_DIM, KS), BF16)
    c_diag = lax.div(t0 + (TQ - 1), KS)

    def sel_step(n, carry):
        c = c_diag - n
        col = pl.ds(pl.multiple_of(c * KS, KS), KS)
        picked = _dot(sel, ex_ref[:, col]) > 0.5
        k_rel = lax.broadcasted_iota(jnp.int32, (1, KS), 1) + (c * KS - t0)
        bias = jnp.where(picked & (k_rel <= t_rel), 0.0, NEG)
        k_t = kvs_ref[0:HEAD_DIM, col].astype(BF16)
        v_aug_t = jnp.concatenate([kvs_ref[HEAD_DIM:GKV, col].astype(BF16), ones], axis=0)
        return _online_update(scores(k_t, k_rel.astype(F32), bias), v_aug_t, *carry)

    init = (jnp.full((GROUP * TQ, 1), NEG, F32), jnp.zeros((GROUP * TQ, 2 * HEAD_DIM), F32))
    o_s = _finish(lax.fori_loop(0, c_diag + 1, sel_step, init)[1])

    gb = gb_ref[...]
    for r in range(GROUP):
        cols = slice(r * HEAD_DIM, (r + 1) * HEAD_DIM)
        rows = slice(r * TQ, (r + 1) * TQ)
        g_s = gb[:, GROUP + r:GROUP + r + 1]
        g_w = gb[:, 2 * GROUP + r:2 * GROUP + r + 1]
        o_ref[:, cols] = oc_ref[:, cols] + g_s * o_s[rows] + g_w * o_w[rows]


def _window_bias():
    t = np.arange(TQ)[None, :, None]
    k = np.arange(WIN_SPAN)[None, None, :] - np.arange(WINDOW // TQ + 1)[:, None, None] * TQ
    dist = t - k
    return jnp.asarray(np.where((dist >= 0) & (dist < WINDOW), 0.0, NEG), dtype=F32)


def _group_expand_matrix(n_keys, n_sel_blk):
    row = np.arange(LANES)[None, :, None]
    kk = np.arange(n_keys)[None, None, :]
    g = np.arange(N_KV)[:, None, None]
    return jnp.asarray(row == g * n_sel_blk + kk // SEL_BLOCK, dtype=BF16)


def _prompt_attn(q, sel, kvs_t, kvw_t, gb, o_c, slopes):
    batch, _, seq = kvs_t.shape
    t = batch * seq
    nt = seq // TQ
    assert seq % KS == 0 and seq >= WIN_SPAN
    expand = _group_expand_matrix(seq, -(-seq // SEL_BLOCK))
    wbias = _window_bias()
    gw = GROUP * HEAD_DIM
    tile = lambda n: pl.BlockSpec((TQ, n), lambda b, g, i: (b * nt + i, g))
    return pl.pallas_call(
        _prompt_attn_kernel,
        grid=(batch, N_KV, nt),
        in_specs=[pl.BlockSpec(memory_space=pltpu.SMEM),
                  tile(gw),
                  pl.BlockSpec((TQ, LANES), lambda b, g, i: (b * nt + i, 0)),
                  pl.BlockSpec((None, GKV, seq), lambda b, g, i: (b, g, 0)),
                  pl.BlockSpec((None, GKV, seq), lambda b, g, i: (b, g, 0)),
                  tile(LANES), tile(gw),
                  pl.BlockSpec((None, LANES, seq), lambda b, g, i: (g, 0, 0)),
                  pl.BlockSpec((None, TQ, WIN_SPAN), lambda b, g, i: (jnp.minimum(i, WINDOW // TQ), 0, 0))],
        out_specs=tile(gw),
        out_shape=jax.ShapeDtypeStruct((t, D_MODEL), F32),
        compiler_params=_params("parallel", "parallel", "arbitrary"),
        name="prompt_attn",
    )(slopes, q, sel, kvs_t, kvw_t, gb, o_c, expand, wbias)


def _stack_heads(q, g):
    return jnp.concatenate([q[:, (g * GROUP + r) * HEAD_DIM:(g * GROUP + r + 1) * HEAD_DIM]
                            for r in range(GROUP)], axis=0).astype(BF16)


def _row_t(t):
    return jnp.concatenate([lax.broadcasted_iota(jnp.int32, (t, 1), 0)] * GROUP, axis=0)


def _row_slopes(slopes_ref, g, t):
    row = lax.broadcasted_iota(jnp.int32, (GROUP * t, 1), 0)
    out = jnp.zeros((GROUP * t, 1), F32)
    for r in range(GROUP):
        out = jnp.where((row >= r * t) & (row < (r + 1) * t), slopes_ref[g * GROUP + r], out)
    return out


def _sample_cmp_kernel(slopes_ref, q_ref, p_ref, hc_ref, w2t_ref, b2_ref, ov_ref, oc_ref, sel_ref,
                       *, n_blk, past_len, blk_step):
    t = q_ref.shape[0]
    q = q_ref[...]
    n_pad = p_ref.shape[0]
    n_idx = lax.broadcasted_iota(jnp.int32, (1, n_pad), 1)
    end_rel = n_idx * CMP_STRIDE + (CMP_LEN - 1) - past_len
    valid = (_row_t(t) >= end_rel) & (n_idx < n_blk)
    end_rel_f = end_rel.astype(F32)
    p_sums = []
    for g in range(N_KV):
        kc, vc = _compress_finish(p_ref[:, g * 4 * CMP_HIDDEN:(g + 1) * 4 * CMP_HIDDEN], hc_ref, w2t_ref, b2_ref)
        s = _dot(_stack_heads(q, g), kc) + _row_slopes(slopes_ref, g, t) * end_rel_f
        p = _softmax_rows(s, valid)
        o = _dot_nt(p.astype(BF16), vc)
        p_sum = None
        for r in range(GROUP):
            h = g * GROUP + r
            oc_ref[:, h * HEAD_DIM:(h + 1) * HEAD_DIM] = o[r * t:(r + 1) * t, :]
            p_sum = p[0:t] if r == 0 else p_sum + p[r * t:(r + 1) * t]
        p_sums.append(p_sum)
    p_all = jnp.concatenate(p_sums, axis=0)
    ov = ov_ref[...]
    imp = sum(_dot(part, ov) for part in _split3(p_all))
    q_pos = jnp.concatenate([lax.broadcasted_iota(jnp.int32, (t, 1), 0)] * N_KV, axis=0) + past_len
    sel = _topk_mask(_block_scores(imp, q_pos, 1), N_SEL, 1)
    sel_ref[...] = jnp.zeros(sel_ref.shape, F32)
    for c in range(sel_ref.shape[0]):
        sel_ref[c, :, 0:blk_step] = sel[:, c * blk_step:(c + 1) * blk_step]


def _sample_cmp(q, pp, hc, w2t, b2, slopes, batch, past_len):
    t = q.shape[0] // batch
    n_ch = pp.shape[1]
    n_blk = n_ch - CMP_PARTS + 1
    n_sel_blk = -(-(past_len + t) // SEL_BLOCK)
    blk_step = SEL_PAGES_PER_STEP * PAGE_SIZE // SEL_BLOCK
    n_chunk = -(-n_sel_blk // blk_step)
    ov = _overlap_matrix(n_blk, n_ch, -(-n_chunk * blk_step // LANES) * LANES, False)
    return pl.pallas_call(
        functools.partial(_sample_cmp_kernel, n_blk=n_blk, past_len=past_len, blk_step=blk_step),
        grid=(batch,),
        in_specs=[pl.BlockSpec(memory_space=pltpu.SMEM),
                  pl.BlockSpec((t, D_MODEL), lambda b: (b, 0)),
                  pl.BlockSpec((None, n_ch, N_KV * 4 * CMP_HIDDEN), lambda b: (b, 0, 0)),
                  _full(hc.shape), _full(w2t.shape), _full(b2.shape), _full(ov.shape)],
        out_specs=[pl.BlockSpec((t, D_MODEL), lambda b: (b, 0)),
                   pl.BlockSpec((None, n_chunk, N_KV * t, LANES), lambda b: (b, 0, 0, 0))],
        out_shape=[jax.ShapeDtypeStruct((batch * t, D_MODEL), F32),
                   jax.ShapeDtypeStruct((batch, n_chunk, N_KV * t, LANES), F32)],
        compiler_params=_params("parallel"),
        name="sample_cmp",
    )(slopes, q, pp, hc, w2t, b2, ov)


def _sample_attn_kernel(*refs, past_len):
    n_pg = SEL_PAGES_PER_STEP
    pg_refs = refs[1:1 + n_pg]
    (slopes_ref, q_ref, selc_ref, sell_ref, ex_ref, kvs_ref, kvw_ref, win_ref, oc_ref, gb_ref,
     o_ref, nwin_ref, m_ref, l_ref, acc_ref) = refs[1 + n_pg:]
    c = pl.program_id(1)
    n_step = pl.num_programs(1)
    t = q_ref.shape[0]
    rows = GROUP * t
    step_keys = n_pg * PAGE_SIZE
    row_t = _row_t(t)

    def update(s, valid, pv, m, l, acc):
        s = jnp.where(valid, s, NEG)
        m_new = jnp.maximum(m, jnp.max(s, axis=-1, keepdims=True))
        p = jnp.exp2(s - m_new)
        alpha = jnp.exp2(m - m_new)
        return m_new, alpha * l + jnp.sum(p, axis=-1, keepdims=True), alpha * acc + pv(p.astype(BF16))

    def finish(l, acc):
        return acc * (1.0 / jnp.maximum(l, 1e-30))

    @pl.when(c == 0)
    def _():
        m_ref[...] = jnp.full(m_ref.shape, NEG, F32)
        l_ref[...] = jnp.zeros(l_ref.shape, F32)
        acc_ref[...] = jnp.zeros(acc_ref.shape, F32)

    q = q_ref[...]
    k_rel = lax.broadcasted_iota(jnp.int32, (1, step_keys), 1) + (c * step_keys - past_len)
    k_rel_f = k_rel.astype(F32)
    expand = ex_ref[...]
    for g in range(N_KV):
        k_t = jnp.concatenate([pr[g, 0] for pr in pg_refs], axis=1).astype(BF16)
        v_t = jnp.concatenate([pr[g, 1] for pr in pg_refs], axis=1).astype(BF16)
        sel = jnp.concatenate([selc_ref[g * t:(g + 1) * t, :]] * GROUP, axis=0).astype(BF16)
        valid = (_dot(sel, expand) > 0.5) & (k_rel <= row_t)
        s = _dot(_stack_heads(q, g), k_t) + _row_slopes(slopes_ref, g, t) * k_rel_f
        m, l, acc = update(s, valid, lambda p: _dot_nt(p, v_t), m_ref[g], l_ref[g], acc_ref[g])
        m_ref[g] = m
        l_ref[g] = l
        acc_ref[g] = acc

    @pl.when(c == n_step - 1)
    def _():
        wb = win_ref.shape[-1]
        pad = jnp.zeros((LANES - t, KV_ROW), F32)
        new_s = jnp.concatenate([kvs_ref[...], pad], axis=0)
        new_w = jnp.concatenate([kvw_ref[...], pad], axis=0)
        i_new = lax.broadcasted_iota(jnp.int32, (1, LANES), 1)
        i_old = lax.broadcasted_iota(jnp.int32, (1, wb), 1) - wb
        gb = gb_ref[...]

        def new_rows_update(new, lo, valid, slope, qs, m, l, acc):
            s = _dot_nt(qs, new[:, lo:lo + HEAD_DIM].astype(BF16)) + slope * i_new.astype(F32)
            v = new[:, lo + HEAD_DIM:lo + GKV].astype(BF16)
            return update(s, valid, lambda p: _dot(p, v), m, l, acc)

        for g in range(N_KV):
            lo = g * GKV
            qs = _stack_heads(q, g)
            slope = _row_slopes(slopes_ref, g, t)
            sel_new = jnp.concatenate([sell_ref[g * t:(g + 1) * t, :]] * GROUP, axis=0)[:, 0:1]
            valid = (sel_new > 0.5) & (i_new <= row_t) & (i_new < t)
            _, l, acc = new_rows_update(new_s, lo, valid, slope, qs, m_ref[g], l_ref[g], acc_ref[g])
            o_s = finish(l, acc)
            dist = row_t - i_old
            valid = (dist >= 0) & (dist < WINDOW) & (i_old + past_len >= 0)
            s = _dot(qs, win_ref[g, 0].astype(BF16)) + slope * i_old.astype(F32)
            v_t = win_ref[g, 1].astype(BF16)
            m, l, acc = update(s, valid, lambda p: _dot_nt(p, v_t), jnp.full((rows, 1), NEG, F32),
                               jnp.zeros((rows, 1), F32), jnp.zeros((rows, HEAD_DIM), F32))
            dist = row_t - i_new
            valid = (dist >= 0) & (dist < WINDOW) & (i_new < t)
            _, l, acc = new_rows_update(new_w, lo, valid, slope, qs, m, l, acc)
            o_w = finish(l, acc)
            for r in range(GROUP):
                h = g * GROUP + r
                gate = [gb[:, g * LANES + br * GROUP + r:g * LANES + br * GROUP + r + 1] for br in range(3)]
                o_ref[:, h * HEAD_DIM:(h + 1) * HEAD_DIM] = (
                    gate[0] * oc_ref[:, h * HEAD_DIM:(h + 1) * HEAD_DIM]
                    + gate[1] * o_s[r * t:(r + 1) * t, :] + gate[2] * o_w[r * t:(r + 1) * t, :])
        new_t = pltpu.roll(new_w.T, LANES - t, 1)
        lane = lax.broadcasted_iota(jnp.int32, (KV_ROW, LANES), 1)
        old = pltpu.roll(win_ref[...].reshape(KV_ROW, wb), wb - t, 1)
        nwin_ref[:, 0:wb - LANES] = old[:, 0:wb - LANES]
        nwin_ref[:, wb - LANES:wb] = jnp.where(lane < LANES - t, old[:, wb - LANES:wb], new_t)


def _sample_attn(q, selc, kvs, kvw, cache_t, win_t, page_table, o_c, gb, slopes, layer, past_len):
    batch, n_pages = page_table.shape
    t = q.shape[0] // batch
    n_step = n_pages // SEL_PAGES_PER_STEP
    wb = win_t.shape[-1]
    expand = _block_expand_matrix(SEL_PAGES_PER_STEP * PAGE_SIZE)
    n_chunk = selc.shape[1]
    tok = lambda n: pl.BlockSpec((t, n), lambda b, c, pt: (b, 0))
    return pl.pallas_call(
        functools.partial(_sample_attn_kernel, past_len=past_len),
        grid_spec=pltpu.PrefetchScalarGridSpec(
            num_scalar_prefetch=1,
            grid=(batch, n_step),
            in_specs=[_page_spec(layer, k, SEL_PAGES_PER_STEP) for k in range(SEL_PAGES_PER_STEP)] + [
                pl.BlockSpec(memory_space=pltpu.SMEM),
                tok(D_MODEL),
                pl.BlockSpec((None, None, N_KV * t, LANES), lambda b, c, pt: (b, c, 0, 0)),
                pl.BlockSpec((None, None, N_KV * t, LANES), lambda b, c, pt: (b, n_chunk - 1, 0, 0)),
                pl.BlockSpec(expand.shape, lambda b, c, pt: (0, 0)),
                tok(KV_ROW), tok(KV_ROW),
                pl.BlockSpec((None, None, N_KV, 2, HEAD_DIM, wb), lambda b, c, pt: (layer, b, 0, 0, 0, 0)),
                tok(D_MODEL), tok(N_KV * LANES)],
            out_specs=[tok(D_MODEL), pl.BlockSpec((None, KV_ROW, wb), lambda b, c, pt: (b, 0, 0))],
            scratch_shapes=[pltpu.VMEM((N_KV, GROUP * t, 1), F32), pltpu.VMEM((N_KV, GROUP * t, 1), F32),
                            pltpu.VMEM((N_KV, GROUP * t, HEAD_DIM), F32)],
        ),
        out_shape=[jax.ShapeDtypeStruct((batch * t, D_MODEL), F32),
                   jax.ShapeDtypeStruct((batch, KV_ROW, wb), F32)],
        compiler_params=_params("parallel", "arbitrary"),
        name="sample_attn",
    )(page_table, *([cache_t] * SEL_PAGES_PER_STEP), slopes, q, selc, selc, expand, kvs, kvw, win_t, o_c, gb)


HIST = 32


def _conv_kernel(z_ref, prev_ref, hist_ref, w_ref, b_ref, y_ref, st_ref, zc_ref, zs_ref):
    i = pl.program_id(1)
    tm = z_ref.shape[0]

    @pl.when(i == 0)
    def _():
        zc_ref[0:HIST, :] = hist_ref[...]

    @pl.when(i > 0)
    def _():
        zc_ref[0:HIST, :] = prev_ref[...]

    zc_ref[HIST:HIST + tm, :] = z_ref[...]
    span = HIST + tm - SUBLANES
    for rho in range(1, SUBLANES):
        zs_ref[rho, 0:span, :] = zc_ref[rho:rho + span, :]
    off = HIST - (CONV_W - 1)
    for cb in range(C_CONV // LANES):
        cols = slice(cb * LANES, (cb + 1) * LANES)
        acc = jnp.zeros((tm, LANES), F32) + b_ref[:, cols]
        for k in range(CONV_W):
            rho = (off + k) % SUBLANES
            base = off + k - rho
            tap = zc_ref[base:base + tm, cols] if rho == 0 else zs_ref[rho, base:base + tm, cols]
            acc = acc + tap * w_ref[k:k + 1, cols]
        y_ref[:, cols] = acc

    @pl.when(i == pl.num_programs(1) - 1)
    def _():
        st_ref[...] = zc_ref[HIST + tm - (CONV_W - 1):HIST + tm, :]


def _conv(z, hist, w, b, batch, tm):
    seq = z.shape[0] // batch
    nt = seq // tm
    per = tm // HIST if tm >= HIST else 1
    z3 = z.reshape(batch, seq, C_CONV)
    if tm >= HIST:
        prev_spec = pl.BlockSpec((None, HIST, C_CONV), lambda b_, i: (b_, jnp.maximum(i * per - 1, 0), 0))
        prev = z3
    else:
        prev_spec = pl.BlockSpec((None, HIST, C_CONV), lambda b_, i: (b_, 0, 0))
        prev = hist
    y, st = pl.pallas_call(
        _conv_kernel,
        grid=(batch, nt),
        in_specs=[pl.BlockSpec((None, tm, C_CONV), lambda b_, i: (b_, i, 0)),
                  prev_spec,
                  pl.BlockSpec((None, HIST, C_CONV), lambda b_, i: (b_, 0, 0)),
                  _full(w.shape), _full(b.shape)],
        out_specs=[pl.BlockSpec((None, tm, C_CONV), lambda b_, i: (b_, i, 0)),
                   pl.BlockSpec((None, CONV_W - 1, C_CONV), lambda b_, i: (b_, 0, 0))],
        out_shape=[jax.ShapeDtypeStruct((batch, seq, C_CONV), F32),
                   jax.ShapeDtypeStruct((batch, CONV_W - 1, C_CONV), F32)],
        scratch_shapes=[pltpu.VMEM((HIST + tm, C_CONV), F32), pltpu.VMEM((SUBLANES, HIST + tm, C_CONV), F32)],
        compiler_params=_params("parallel", "arbitrary"),
        name="conv",
    )(z3, prev, hist, w, b)
    return y.reshape(batch * seq, C_CONV), st


def _post_kernel(x_ref, y_ref, oa_ref, gm_ref, ple_ref,
                 wpw_ref, wout_ref, wg_ref, wu_ref, wd_ref, wpg_ref, wpp_ref,
                 cg_ref, cb_ref, bpw_ref, g1_ref, b1_ref, g2_ref, b2_ref, o_ref, *, alpha):
    yn = _layernorm(y_ref[...], cg_ref[...], cb_ref[...])
    o_b = _dot((yn * _sigmoid(yn)).astype(BF16), wpw_ref[...]) + bpw_ref[...]
    mixed = gm_ref[:, 0:D_MODEL] * oa_ref[...] + gm_ref[:, D_MODEL:2 * D_MODEL] * o_b
    x1 = _layernorm(alpha * x_ref[...] + _dot(mixed.astype(BF16), wout_ref[...]), g1_ref[...], b1_ref[...])
    x1b = x1.astype(BF16)
    hg = _dot(x1b, wg_ref[...])
    hu = _dot(x1b, wu_ref[...])
    f = _dot((hg * _sigmoid(hg) * hu).astype(BF16), wd_ref[...])
    x2 = _layernorm(alpha * x1 + f, g2_ref[...], b2_ref[...])
    gate = _sigmoid(_dot(x2.astype(BF16), wpg_ref[...]))
    o_ref[...] = x2 + gate * _dot(ple_ref[...].astype(BF16), wpp_ref[...])


def _post(x, y, o_a, gm, ple, w, alpha, tm):
    t = x.shape[0]
    row = lambda n: pl.BlockSpec((tm, n), lambda i: (i, 0))
    mats = [w["w_pw2"], w["w_out"], w["w_ffn_gate"], w["w_ffn_up"], w["w_ffn_down"], w["w_ple_gate"], w["w_ple_in"]]
    vecs = [w["conv_ln_g"], w["conv_ln_b"], w["b_pw2"], w["ln1_g"], w["ln1_b"], w["ln2_g"], w["ln2_b"]]
    return pl.pallas_call(
        functools.partial(_post_kernel, alpha=alpha),
        grid=(t // tm,),
        in_specs=[row(D_MODEL), row(C_CONV), row(D_MODEL), row(2 * D_MODEL), row(ple.shape[1])]
                 + [_resident(m.shape) for m in mats] + [_resident(v.shape) for v in vecs],
        out_specs=row(D_MODEL),
        out_shape=jax.ShapeDtypeStruct((t, D_MODEL), F32),
        compiler_params=_params("parallel"),
        name="post",
    )(x, y, o_a, gm, ple, *mats, *vecs)


def _prep_layer(l, w_in, b_in, cmp_w1, cmp_w2, cmp_b2, named):
    offs = np.cumsum((D_MODEL, 3 * KV_ROW, 3 * N_HEADS, 2 * C_CONV, 2 * D_MODEL))
    wl, bl = w_in[l], b_in[l]
    qs = ATTN_SCALE * LOG2E
    new_cols = np.array([g * LANES + br * GROUP + r for br in range(3) for g in range(N_KV) for r in range(GROUP)])
    wgb = jnp.zeros((D_MODEL, N_KV * LANES), F32).at[:, new_cols].set(wl[:, offs[1]:offs[2]])
    bgb = jnp.zeros((N_KV * LANES,), F32).at[new_cols].set(bl[offs[1]:offs[2]])
    out = {
        "wq": (wl[:, :offs[0]] * qs).astype(BF16), "bq": (bl[:offs[0]] * qs)[None],
        "wkvT": wl[:, offs[0]:offs[1]].T.astype(BF16),
        "bkv_row": bl[offs[0]:offs[1]][None], "bkv_col": bl[offs[0]:offs[1]][:, None],
        "wgb": wgb.astype(BF16), "bgb": bgb[None],
        "wu": wl[:, offs[2]:offs[3]].astype(BF16), "bu": bl[offs[2]:offs[3]][None],
        "wgm": wl[:, offs[3]:offs[4]].astype(BF16), "bgm": bl[offs[3]:offs[4]][None],
    }
    w1 = cmp_w1[l].reshape(2, CMP_PARTS, CMP_STRIDE, HEAD_DIM, CMP_HIDDEN).transpose(2, 0, 3, 1, 4)
    zero = jnp.zeros_like(w1[:, 0])
    w1cat = jnp.stack([jnp.stack([w1[:, 0], zero], axis=2), jnp.stack([zero, w1[:, 1]], axis=2)], axis=1)
    out["w1cat"] = w1cat.reshape(CMP_STRIDE * 2 * HEAD_DIM, 4 * CMP_HIDDEN).astype(BF16)
    out["w2t"] = jnp.transpose(cmp_w2[l], (0, 2, 1)).astype(BF16)
    out["b2"] = cmp_b2[l][:, :, None]
    for name, val in named.items():
        v = val[l]
        out[name] = v.astype(BF16) if v.ndim == 2 else v[None]
    return out


def _kv_out(stacked_t):
    l, b, _, t = stacked_t.shape
    return jnp.transpose(stacked_t.reshape(l, b, N_KV, 2, HEAD_DIM, t), (0, 1, 5, 2, 3, 4))


def kernel(x_prompt, x_sample, cache_cmp, cache_sel, state_win, state_conv, page_table, p_prompt, p_sample,
           w_in, b_in, cmp_w1, cmp_pe, cmp_b1, cmp_w2, cmp_b2, conv_w, conv_b, conv_ln_g, conv_ln_b,
           w_pw2, b_pw2, w_out, ln1_g, ln1_b, w_ffn_gate, w_ffn_up, w_ffn_down, ln2_g, ln2_b,
           w_ple_in, w_ple_gate):
    depth = w_in.shape[0]
    alpha = float((2 * depth) ** 0.25)
    batch, seq = x_prompt.shape[:2]
    dbatch, dseq = x_sample.shape[:2]
    past_len = page_table.shape[1] * PAGE_SIZE
    slopes = jnp.exp2(-8.0 * jnp.arange(1, N_HEADS + 1, dtype=F32) / N_HEADS) * LOG2E
    hconst = _pe_term(cmp_pe, cmp_w1, cmp_b1)
    named = dict(w_pw2=w_pw2, w_out=w_out, w_ffn_gate=w_ffn_gate, w_ffn_up=w_ffn_up, w_ffn_down=w_ffn_down,
                 w_ple_gate=w_ple_gate, w_ple_in=w_ple_in, conv_ln_g=conv_ln_g, conv_ln_b=conv_ln_b,
                 b_pw2=b_pw2, ln1_g=ln1_g, ln1_b=ln1_b, ln2_g=ln2_g, ln2_b=ln2_b, conv_b=conv_b)
    layers = [_prep_layer(l, w_in, b_in, cmp_w1, cmp_w2, cmp_b2, named) for l in range(depth)]

    x = x_prompt.reshape(batch * seq, D_MODEL)
    zero_hist = jnp.zeros((batch, HIST, C_CONV), F32)
    out_p = [[], [], [], []]
    for l, w in enumerate(layers):
        q, kvc_t, kvs_t, kvw_t, gb, z, gm = _in_proj(x, w, batch, BF16, 256, True)
        pp = _cmp_partial_prompt(kvc_t, w["w1cat"])
        o_c, sel = _prompt_select(q, pp, gb, hconst[l], w["w2t"], w["b2"], slopes, batch)
        o_a = _prompt_attn(q, sel, kvs_t, kvw_t, gb, o_c, slopes)
        y, st = _conv(z, zero_hist, conv_w[l], w["conv_b"], batch, 256)
        x = _post(x, y, o_a, gm, p_prompt[l].reshape(batch * seq, -1), w, alpha, 256)
        out_p[0].append(kvc_t)
        out_p[1].append(kvs_t)
        out_p[2].append(kvw_t[:, :, seq - min(WINDOW, seq):])
        out_p[3].append(st)
    y_prompt = x.reshape(batch, seq, D_MODEL)

    x = x_sample.reshape(dbatch * dseq, D_MODEL)
    cmp_t = _paged_view(cache_cmp)
    sel_t = _paged_view(cache_sel)
    win_t = jnp.transpose(state_win, (0, 1, 3, 4, 5, 2))
    out_s = [[], [], [], []]
    for l, w in enumerate(layers):
        q, kvc, kvs, kvw, gb, z, gm = _in_proj(x, w, 1, F32, dbatch * dseq, False)
        pp = _cmp_partial_sample(cmp_t, page_table, w["w1cat"], l)
        o_c, selc = _sample_cmp(q, pp, hconst[l], w["w2t"], w["b2"], slopes, dbatch, past_len)
        o_a, nwin = _sample_attn(q, selc, kvs, kvw, sel_t, win_t, page_table, o_c, gb, slopes, l, past_len)
        hist = jnp.pad(state_conv[l], ((0, 0), (HIST - (CONV_W - 1), 0), (0, 0)))
        y, st = _conv(z, hist, conv_w[l], w["conv_b"], dbatch, dseq)
        x = _post(x, y, o_a, gm, p_sample[l].reshape(dbatch * dseq, -1), w, alpha, dbatch * dseq)
        out_s[0].append(kvc.reshape(dbatch, dseq, N_KV, 2, HEAD_DIM))
        out_s[1].append(kvs.reshape(dbatch, dseq, N_KV, 2, HEAD_DIM))
        out_s[2].append(nwin)
        out_s[3].append(st)
    y_sample = x.reshape(dbatch, dseq, D_MODEL)

    return (y_prompt, y_sample, _kv_out(jnp.stack(out_p[0])), _kv_out(jnp.stack(out_p[1])),
            _kv_out(jnp.stack(out_p[2])), jnp.stack(out_p[3]),
            jnp.stack(out_s[0]), jnp.stack(out_s[1]), _kv_out(jnp.stack(out_s[2])), jnp.stack(out_s[3]))
```

```python
import functools
import math

import numpy as np
import jax
import jax.numpy as jnp
from jax import lax
from jax.experimental import pallas as pl
from jax.experimental.pallas import tpu as pltpu

F32 = jnp.float32
BF16 = jnp.bfloat16

D_MODEL = 1024
N_HEADS = 16
HEAD_DIM = 64
N_KV = 4
GROUP = N_HEADS // N_KV
KV_ROW = N_KV * 2 * HEAD_DIM
GKV = 2 * HEAD_DIM
CMP_LEN = 32
CMP_STRIDE = 16
CMP_PARTS = CMP_LEN // CMP_STRIDE
CMP_HIDDEN = 2 * HEAD_DIM
SEL_BLOCK = 64
SEL_SHIFT = SEL_BLOCK.bit_length() - 1
N_SEL = 16
WINDOW = 512
C_CONV = D_MODEL
CONV_W = 31
PAGE_SIZE = 128
CH_PAGE = PAGE_SIZE // CMP_STRIDE
LN_EPS = 1e-5
ATTN_SCALE = HEAD_DIM ** -0.5
LOG2E = math.log2(math.e)
FORCE_SCORE = 1e4
NEG = -1e30
LANES = 128
SUBLANES = 8
VMEM_LIMIT = 56 * 1024 * 1024

PAGES_PER_STEP = 16
SEL_PAGES_PER_STEP = 64
TQS = 256
TQ = 256
KS = 512


def _dot(a, b):
    return jnp.dot(a, b, preferred_element_type=F32)


def _dot_nt(a, b):
    return lax.dot_general(a, b, (((1,), (1,)), ((), ())), preferred_element_type=F32)


def _split3(a):
    a1 = a.astype(BF16)
    r1 = a - a1.astype(F32)
    a2 = r1.astype(BF16)
    return a1, a2, (r1 - a2.astype(F32)).astype(BF16)


def _sigmoid(x):
    return 1.0 / (1.0 + jnp.exp(-x))


def _layernorm(x, g, b):
    mu = jnp.mean(x, axis=-1, keepdims=True)
    xc = x - mu
    var = jnp.mean(xc * xc, axis=-1, keepdims=True)
    return xc * lax.rsqrt(var + LN_EPS) * g + b


def _params(*sem):
    return pltpu.CompilerParams(dimension_semantics=sem, vmem_limit_bytes=VMEM_LIMIT)


def _full(shape):
    n = len(shape)
    return pl.BlockSpec(shape, lambda *_: (0,) * n)


def _resident(shape):
    n = len(shape)
    return pl.BlockSpec(shape, lambda *_: (0,) * n, pipeline_mode=pl.Buffered(1))


def _pe_term_kernel(pe_ref, w1_ref, b1_ref, o_ref):
    o_ref[...] = b1_ref[...] + jnp.sum(pe_ref[...] * w1_ref[...], axis=0, keepdims=True)


def _pe_term(cmp_pe, cmp_w1, cmp_b1):
    n = cmp_pe.shape[0] * 2
    k = CMP_LEN * HEAD_DIM
    out = pl.pallas_call(
        _pe_term_kernel,
        grid=(n,),
        in_specs=[pl.BlockSpec((None, k, 1), lambda i: (i, 0, 0)),
                  pl.BlockSpec((None, k, CMP_HIDDEN), lambda i: (i, 0, 0)),
                  pl.BlockSpec((None, 1, CMP_HIDDEN), lambda i: (i, 0, 0))],
        out_specs=pl.BlockSpec((None, 1, CMP_HIDDEN), lambda i: (i, 0, 0)),
        out_shape=jax.ShapeDtypeStruct((n, 1, CMP_HIDDEN), F32),
        compiler_params=_params("parallel"),
        name="pe_term",
    )(cmp_pe.reshape(n, k, 1), cmp_w1.reshape(n, k, CMP_HIDDEN), cmp_b1.reshape(n, 1, CMP_HIDDEN))
    return out.reshape(-1, 2, CMP_HIDDEN)


def _in_proj_kernel(x_ref, wq_ref, wkv_ref, wgb_ref, wu_ref, wgm_ref,
                    bq_ref, bkv_ref, bgb_ref, bu_ref, bgm_ref,
                    q_ref, kvc_ref, kvs_ref, kvw_ref, gb_ref, z_ref, gm_ref, *, kv_t):
    xb = x_ref[...].astype(BF16)
    q_ref[...] = (_dot(xb, wq_ref[...]) + bq_ref[...]).astype(q_ref.dtype)
    if kv_t:
        kv = _dot_nt(wkv_ref[...], xb) + bkv_ref[...]
        kvc_ref[...] = kv[0:KV_ROW, :]
        kvs_ref[...] = kv[KV_ROW:2 * KV_ROW, :]
        kvw_ref[...] = kv[2 * KV_ROW:3 * KV_ROW, :]
    else:
        kv = _dot_nt(xb, wkv_ref[...]) + bkv_ref[...]
        kvc_ref[...] = kv[:, 0:KV_ROW]
        kvs_ref[...] = kv[:, KV_ROW:2 * KV_ROW]
        kvw_ref[...] = kv[:, 2 * KV_ROW:3 * KV_ROW]
    gb_ref[...] = _sigmoid(_dot(xb, wgb_ref[...]) + bgb_ref[...])
    u = _dot(xb, wu_ref[...]) + bu_ref[...]
    z_ref[...] = u[:, 0:C_CONV] * _sigmoid(u[:, C_CONV:2 * C_CONV])
    gm_ref[...] = _sigmoid(_dot(xb, wgm_ref[...]) + bgm_ref[...])


def _in_proj(x, w, batch, q_dtype, tm, kv_t):
    t = x.shape[0]
    nt = t // batch // tm
    row = lambda n: pl.BlockSpec((tm, n), lambda b, i: (b * nt + i, 0))
    mats = [w["wq"], w["wkvT"], w["wgb"], w["wu"], w["wgm"]]
    vecs = [w["bq"], w["bkv_col"] if kv_t else w["bkv_row"], w["bgb"], w["bu"], w["bgm"]]
    if kv_t:
        kv_spec = pl.BlockSpec((None, KV_ROW, tm), lambda b, i: (b, 0, i))
        kv_shape = jax.ShapeDtypeStruct((batch, KV_ROW, t // batch), F32)
    else:
        kv_spec = row(KV_ROW)
        kv_shape = jax.ShapeDtypeStruct((t, KV_ROW), F32)
    return pl.pallas_call(
        functools.partial(_in_proj_kernel, kv_t=kv_t),
        grid=(batch, nt),
        in_specs=[row(D_MODEL)] + [_resident(m.shape) for m in mats] + [_resident(v.shape) for v in vecs],
        out_specs=[row(D_MODEL), kv_spec, kv_spec, kv_spec, row(N_KV * LANES), row(C_CONV), row(2 * D_MODEL)],
        out_shape=[jax.ShapeDtypeStruct((t, D_MODEL), q_dtype), kv_shape, kv_shape, kv_shape,
                   jax.ShapeDtypeStruct((t, N_KV * LANES), F32),
                   jax.ShapeDtypeStruct((t, C_CONV), F32),
                   jax.ShapeDtypeStruct((t, 2 * D_MODEL), F32)],
        compiler_params=_params("parallel", "parallel"),
        name="in_proj",
    )(x, *mats, *vecs)


def _cmp_partial_body(page_fn, n_pages, perm_ref, w_ref, p_ref):
    perm = perm_ref[...]
    regrouped = []
    for g in range(N_KV):
        pages_t = jnp.concatenate([page_fn(k, g).T for k in range(n_pages)], axis=1).astype(BF16)
        rg = _dot(perm, pages_t)
        regrouped.append([rg[:, k * GKV:(k + 1) * GKV] for k in range(n_pages)])
    a = jnp.concatenate(
        [jnp.concatenate([regrouped[g][k][j * CH_PAGE:(j + 1) * CH_PAGE, :]
                          for g in range(N_KV) for k in range(n_pages)], axis=0).astype(BF16)
         for j in range(CMP_STRIDE)], axis=1)
    acc = _dot(a, w_ref[...])
    rows = n_pages * CH_PAGE
    for g in range(N_KV):
        p_ref[:, g * 4 * CMP_HIDDEN:(g + 1) * 4 * CMP_HIDDEN] = acc[g * rows:(g + 1) * rows, :]


def _cmp_partial_prompt_kernel(x_ref, perm_ref, w_ref, p_ref):
    n_pages = x_ref.shape[1] // PAGE_SIZE
    page_fn = lambda k, g: x_ref[g * GKV:(g + 1) * GKV, k * PAGE_SIZE:(k + 1) * PAGE_SIZE]
    _cmp_partial_body(page_fn, n_pages, perm_ref, w_ref, p_ref)


def _cmp_partial_sample_kernel(*refs):
    pg_refs = refs[1:1 + PAGES_PER_STEP]
    perm_ref, w_ref, p_ref = refs[1 + PAGES_PER_STEP:]
    page_fn = lambda k, g: pg_refs[k][g].reshape(GKV, PAGE_SIZE)
    _cmp_partial_body(page_fn, PAGES_PER_STEP, perm_ref, w_ref, p_ref)


def _row_regroup_matrix():
    out = np.zeros((PAGE_SIZE, PAGE_SIZE), np.float32)
    for j in range(CMP_STRIDE):
        for c in range(CH_PAGE):
            out[j * CH_PAGE + c, c * CMP_STRIDE + j] = 1.0
    return jnp.asarray(out, dtype=BF16)


def _cmp_partial_prompt(kvc_t, w1cat):
    batch, _, seq = kvc_t.shape
    n_ch = seq // CMP_STRIDE
    perm = _row_regroup_matrix()
    return pl.pallas_call(
        _cmp_partial_prompt_kernel,
        grid=(batch,),
        in_specs=[pl.BlockSpec((None, KV_ROW, seq), lambda b: (b, 0, 0)),
                  _resident(perm.shape), _resident(w1cat.shape)],
        out_specs=pl.BlockSpec((None, n_ch, N_KV * 4 * CMP_HIDDEN), lambda b: (b, 0, 0)),
        out_shape=jax.ShapeDtypeStruct((batch, n_ch, N_KV * 4 * CMP_HIDDEN), F32),
        compiler_params=_params("parallel"),
        name="cmp_partial_prompt",
    )(kvc_t, perm, w1cat)


def _paged_view(cache):
    return jnp.transpose(cache, (0, 1, 3, 4, 5, 2))


def _page_spec(layer, k, per_step):
    return pl.BlockSpec((None, None, N_KV, 2, HEAD_DIM, PAGE_SIZE),
                        lambda b, c, pt: (layer, pt[b, c * per_step + k], 0, 0, 0, 0))


def _cmp_partial_sample(cache_t, page_table, w1cat, layer):
    batch, n_pages = page_table.shape
    n_step = n_pages // PAGES_PER_STEP
    rows = PAGES_PER_STEP * CH_PAGE
    perm = _row_regroup_matrix()
    const = lambda shape: pl.BlockSpec(shape, lambda b, c, pt: (0,) * len(shape), pipeline_mode=pl.Buffered(1))
    return pl.pallas_call(
        _cmp_partial_sample_kernel,
        grid_spec=pltpu.PrefetchScalarGridSpec(
            num_scalar_prefetch=1,
            grid=(batch, n_step),
            in_specs=[_page_spec(layer, k, PAGES_PER_STEP) for k in range(PAGES_PER_STEP)]
                     + [const(perm.shape), const(w1cat.shape)],
            out_specs=pl.BlockSpec((None, rows, N_KV * 4 * CMP_HIDDEN), lambda b, c, pt: (b, c, 0)),
        ),
        out_shape=jax.ShapeDtypeStruct((batch, n_step * rows, N_KV * 4 * CMP_HIDDEN), F32),
        compiler_params=_params("parallel", "parallel"),
        name="cmp_partial_sample",
    )(page_table, *([cache_t] * PAGES_PER_STEP), perm, w1cat)


def _compress_finish(pb, hc_ref, w2t_ref, b2_ref):
    n = pb.shape[0]
    out = []
    for s in range(2):
        p0 = pb[:, s * 2 * CMP_HIDDEN:s * 2 * CMP_HIDDEN + CMP_HIDDEN]
        p1 = pb[:, s * 2 * CMP_HIDDEN + CMP_HIDDEN:(s + 1) * 2 * CMP_HIDDEN]
        h = hc_ref[s:s + 1, :] + p0 + pltpu.roll(p1, n - 1, 0)
        a = h * _sigmoid(h)
        out.append((_dot_nt(w2t_ref[s], a.astype(BF16)) + b2_ref[s]).astype(BF16))
    return out


def _softmax_rows(s, valid):
    s = jnp.where(valid, s, NEG)
    m = jnp.max(s, axis=-1, keepdims=True)
    e = jnp.where(valid, jnp.exp2(s - m), 0.0)
    return e * (1.0 / jnp.maximum(jnp.sum(e, axis=-1, keepdims=True), 1e-30))


def _topk_mask(score, n_top, axis):
    idx = lax.broadcasted_iota(jnp.int32, score.shape, axis).astype(F32)
    big = float(score.shape[axis])
    sel = jnp.zeros(score.shape, F32)
    for _ in range(n_top):
        m = jnp.max(score, axis=axis, keepdims=True)
        first = jnp.min(jnp.where(score == m, idx, big), axis=axis, keepdims=True)
        hit = idx == first
        sel = jnp.where(hit & (m > -0.5), 1.0, sel)
        score = jnp.where(hit, -3e38, score)
    return sel


def _block_scores(imp, q_pos, axis):
    j = lax.broadcasted_iota(jnp.int32, imp.shape, axis)
    cur = jnp.right_shift(q_pos, SEL_SHIFT)
    forced = (j == 0) | (j == cur) | (j == cur - 1)
    return jnp.where(j > cur, -1.0, jnp.where(forced, FORCE_SCORE, imp))


def _online_update(s, v_aug_t, m, acc):
    m_new = jnp.maximum(m, jnp.max(s, axis=-1, keepdims=True))
    p = jnp.exp2(s - m_new).astype(BF16)
    acc = jnp.exp2(m - m_new) * acc + _dot_nt(p, v_aug_t)
    return m_new, acc


def _finish(acc):
    return acc[:, 0:HEAD_DIM] * (1.0 / jnp.maximum(acc[:, HEAD_DIM:2 * HEAD_DIM], 1e-30))


def _overlap_matrix(n_blk, n_blk_pad, n_sel_pad, transpose):
    n = np.arange(n_blk_pad)[:, None]
    j = np.arange(n_sel_pad)[None, :]
    ov = (n * CMP_STRIDE < j * SEL_BLOCK + SEL_BLOCK) & (n * CMP_STRIDE + CMP_LEN > j * SEL_BLOCK) & (n < n_blk)
    return jnp.asarray(ov.T if transpose else ov, dtype=BF16)


def _block_expand_matrix(n_keys):
    jj = np.arange(LANES)[:, None]
    kk = np.arange(n_keys)[None, :]
    return jnp.asarray(jj == kk // SEL_BLOCK, dtype=BF16)


def _prompt_select_kernel(slopes_ref, q_ref, p_ref, gb_ref, hc_ref, w2t_ref, b2_ref, ovt_ref,
                          oc_ref, sel_ref, kc_ref, vc_ref, *, n_blk, n_sel_blk):
    i = pl.program_id(1)
    t0 = i * TQS

    @pl.when(i == 0)
    def _():
        for g in range(N_KV):
            kc, vc = _compress_finish(p_ref[:, g * 4 * CMP_HIDDEN:(g + 1) * 4 * CMP_HIDDEN],
                                      hc_ref, w2t_ref, b2_ref)
            kc_ref[g] = kc
            vc_ref[g] = vc

    q = q_ref[...]
    gb = gb_ref[...]
    t_rel = lax.broadcasted_iota(jnp.int32, (TQS, 1), 0)
    n_pad = kc_ref.shape[2]
    n_idx = lax.broadcasted_iota(jnp.int32, (1, n_pad), 1)
    end_rel = n_idx * CMP_STRIDE + (CMP_LEN - 1) - t0
    valid_c = (t_rel >= end_rel) & (n_idx < n_blk)
    end_rel_f = end_rel.astype(F32)
    ovt = ovt_ref[...]
    imp_parts = []
    for g in range(N_KV):
        kc = kc_ref[g]
        vc = vc_ref[g]
        p_sum = None
        for r in range(GROUP):
            h = g * GROUP + r
            s = _dot(q[:, h * HEAD_DIM:(h + 1) * HEAD_DIM], kc) + slopes_ref[h] * end_rel_f
            p = _softmax_rows(s, valid_c)
            gate = gb[:, g * LANES + r:g * LANES + r + 1]
            oc_ref[:, h * HEAD_DIM:(h + 1) * HEAD_DIM] = gate * _dot_nt(p.astype(BF16), vc)
            p_sum = p if p_sum is None else p_sum + p
        imp_parts.append(sum(_dot_nt(ovt, part) for part in _split3(p_sum))[0:n_sel_blk, :])
    imp_t = jnp.stack(imp_parts, axis=0)
    q_pos = lax.broadcasted_iota(jnp.int32, (1, 1, TQS), 2) + t0
    sel_t = _topk_mask(_block_scores(imp_t, q_pos, 1), N_SEL, 1).reshape(N_KV * n_sel_blk, TQS)
    sel_ref[...] = sel_t.T.astype(BF16)


def _prompt_select(q, pp, gb, hc, w2t, b2, slopes, batch):
    t = q.shape[0]
    seq = t // batch
    nt = seq // TQS
    n_ch = pp.shape[1]
    n_blk = n_ch - CMP_PARTS + 1
    n_sel_blk = -(-seq // SEL_BLOCK)
    assert N_KV * n_sel_blk == LANES
    ovt = _overlap_matrix(n_blk, n_ch, LANES, True)
    row = lambda n: pl.BlockSpec((TQS, n), lambda b, i: (b * nt + i, 0))
    return pl.pallas_call(
        functools.partial(_prompt_select_kernel, n_blk=n_blk, n_sel_blk=n_sel_blk),
        grid=(batch, nt),
        in_specs=[pl.BlockSpec(memory_space=pltpu.SMEM),
                  row(D_MODEL),
                  pl.BlockSpec((None, n_ch, N_KV * 4 * CMP_HIDDEN), lambda b, i: (b, 0, 0)),
                  row(N_KV * LANES),
                  _full(hc.shape), _full(w2t.shape), _full(b2.shape), _full(ovt.shape)],
        out_specs=[row(D_MODEL), row(LANES)],
        out_shape=[jax.ShapeDtypeStruct((t, D_MODEL), F32), jax.ShapeDtypeStruct((t, LANES), BF16)],
        scratch_shapes=[pltpu.VMEM((N_KV, HEAD_DIM, n_ch), BF16), pltpu.VMEM((N_KV, HEAD_DIM, n_ch), BF16)],
        compiler_params=_params("parallel", "arbitrary"),
        name="prompt_select",
    )(slopes, q, pp, gb, hc, w2t, b2, ovt)


WIN_SPAN = WINDOW + TQ


def _prompt_attn_kernel(slopes_ref, q_ref, sel_ref, kvs_ref, kvw_ref, gb_ref, oc_ref, ex_ref, wbias_ref, o_ref):
    g = pl.program_id(1)
    i = pl.program_id(2)
    t0 = i * TQ
    q = q_ref[...]
    q_all = jnp.concatenate([q[:, r * HEAD_DIM:(r + 1) * HEAD_DIM] for r in range(GROUP)], axis=0)
    slopes = [slopes_ref[g * GROUP + r] for r in range(GROUP)]
    t_rel = lax.broadcasted_iota(jnp.int32, (TQ, 1), 0)

    def scores(k_t, k_rel_f, bias):
        return jnp.concatenate([_dot(q_all[r * TQ:(r + 1) * TQ], k_t) + (slopes[r] * k_rel_f + bias)
                                for r in range(GROUP)], axis=0)

    start = pl.multiple_of(jnp.maximum(t0 - WINDOW, 0), TQ)
    k_t = kvw_ref[0:HEAD_DIM, pl.ds(start, WIN_SPAN)].astype(BF16)
    v_aug_t = jnp.concatenate([kvw_ref[HEAD_DIM:GKV, pl.ds(start, WIN_SPAN)].astype(BF16),
                               jnp.ones((HEAD_DIM, WIN_SPAN), BF16)], axis=0)
    k_rel_f = (lax.broadcasted_iota(jnp.int32, (1, WIN_SPAN), 1) + (start - t0)).astype(F32)
    s = scores(k_t, k_rel_f, wbias_ref[...])
    p = jnp.exp2(s - jnp.max(s, axis=-1, keepdims=True)).astype(BF16)
    o_w = _finish(_dot_nt(p, v_aug_t))

    sel = sel_ref[...]
    ones = jnp.ones((HEAD_DIM, KS), BF16)
    c_diag = lax.div(t0 + (TQ - 1), KS)

    def sel_step(n, carry):
        c = c_diag - n
        col = pl.ds(pl.multiple_of(c * KS, KS), KS)
        picked = _dot(sel, ex_ref[:, col]) > 0.5
        k_rel = lax.broadcasted_iota(jnp.int32, (1, KS), 1) + (c * KS - t0)
        bias = jnp.where(picked & (k_rel <= t_rel), 0.0, NEG)
        k_t = kvs_ref[0:HEAD_DIM, col].astype(BF16)
        v_aug_t = jnp.concatenate([kvs_ref[HEAD_DIM:GKV, col].astype(BF16), ones], axis=0)
        return _online_update(scores(k_t, k_rel.astype(F32), bias), v_aug_t, *carry)

    init = (jnp.full((GROUP * TQ, 1), NEG, F32), jnp.zeros((GROUP * TQ, 2 * HEAD_DIM), F32))
    o_s = _finish(lax.fori_loop(0, c_diag + 1, sel_step, init)[1])

    gb = gb_ref[...]
    for r in range(GROUP):
        cols = slice(r * HEAD_DIM, (r + 1) * HEAD_DIM)
        rows = slice(r * TQ, (r + 1) * TQ)
        g_s = gb[:, GROUP + r:GROUP + r + 1]
        g_w = gb[:, 2 * GROUP + r:2 * GROUP + r + 1]
        o_ref[:, cols] = oc_ref[:, cols] + g_s * o_s[rows] + g_w * o_w[rows]


def _window_bias():
    t = np.arange(TQ)[None, :, None]
    k = np.arange(WIN_SPAN)[None, None, :] - np.arange(WINDOW // TQ + 1)[:, None, None] * TQ
    dist = t - k
    return jnp.asarray(np.where((dist >= 0) & (dist < WINDOW), 0.0, NEG), dtype=F32)


def _group_expand_matrix(n_keys, n_sel_blk):
    row = np.arange(LANES)[None, :, None]
    kk = np.arange(n_keys)[None, None, :]
    g = np.arange(N_KV)[:, None, None]
    return jnp.asarray(row == g * n_sel_blk + kk // SEL_BLOCK, dtype=BF16)


def _prompt_attn(q, sel, kvs_t, kvw_t, gb, o_c, slopes):
    batch, _, seq = kvs_t.shape
    t = batch * seq
    nt = seq // TQ
    assert seq % KS == 0 and seq >= WIN_SPAN
    expand = _group_expand_matrix(seq, -(-seq // SEL_BLOCK))
    wbias = _window_bias()
    gw = GROUP * HEAD_DIM
    tile = lambda n: pl.BlockSpec((TQ, n), lambda b, g, i: (b * nt + i, g))
    return pl.pallas_call(
        _prompt_attn_kernel,
        grid=(batch, N_KV, nt),
        in_specs=[pl.BlockSpec(memory_space=pltpu.SMEM),
                  tile(gw),
                  pl.BlockSpec((TQ, LANES), lambda b, g, i: (b * nt + i, 0)),
                  pl.BlockSpec((None, GKV, seq), lambda b, g, i: (b, g, 0)),
                  pl.BlockSpec((None, GKV, seq), lambda b, g, i: (b, g, 0)),
                  tile(LANES), tile(gw),
                  pl.BlockSpec((None, LANES, seq), lambda b, g, i: (g, 0, 0)),
                  pl.BlockSpec((None, TQ, WIN_SPAN), lambda b, g, i: (jnp.minimum(i, WINDOW // TQ), 0, 0))],
        out_specs=tile(gw),
        out_shape=jax.ShapeDtypeStruct((t, D_MODEL), F32),
        compiler_params=_params("parallel", "parallel", "arbitrary"),
        name="prompt_attn",
    )(slopes, q, sel, kvs_t, kvw_t, gb, o_c, expand, wbias)


def _stack_heads(q, g):
    return jnp.concatenate([q[:, (g * GROUP + r) * HEAD_DIM:(g * GROUP + r + 1) * HEAD_DIM]
                            for r in range(GROUP)], axis=0).astype(BF16)


def _row_t(t):
    return jnp.concatenate([lax.broadcasted_iota(jnp.int32, (t, 1), 0)] * GROUP, axis=0)


def _row_slopes(slopes_ref, g, t):
    row = lax.broadcasted_iota(jnp.int32, (GROUP * t, 1), 0)
    out = jnp.zeros((GROUP * t, 1), F32)
    for r in range(GROUP):
        out = jnp.where((row >= r * t) & (row < (r + 1) * t), slopes_ref[g * GROUP + r], out)
    return out


def _sample_cmp_kernel(slopes_ref, q_ref, p_ref, hc_ref, w2t_ref, b2_ref, ov_ref, oc_ref, sel_ref,
                       *, n_blk, past_len, blk_step):
    t = q_ref.shape[0]
    q = q_ref[...]
    n_pad = p_ref.shape[0]
    n_idx = lax.broadcasted_iota(jnp.int32, (1, n_pad), 1)
    end_rel = n_idx * CMP_STRIDE + (CMP_LEN - 1) - past_len
    valid = (_row_t(t) >= end_rel) & (n_idx < n_blk)
    end_rel_f = end_rel.astype(F32)
    p_sums = []
    for g in range(N_KV):
        kc, vc = _compress_finish(p_ref[:, g * 4 * CMP_HIDDEN:(g + 1) * 4 * CMP_HIDDEN], hc_ref, w2t_ref, b2_ref)
        s = _dot(_stack_heads(q, g), kc) + _row_slopes(slopes_ref, g, t) * end_rel_f
        p = _softmax_rows(s, valid)
        o = _dot_nt(p.astype(BF16), vc)
        p_sum = None
        for r in range(GROUP):
            h = g * GROUP + r
            oc_ref[:, h * HEAD_DIM:(h + 1) * HEAD_DIM] = o[r * t:(r + 1) * t, :]
            p_sum = p[0:t] if r == 0 else p_sum + p[r * t:(r + 1) * t]
        p_sums.append(p_sum)
    p_all = jnp.concatenate(p_sums, axis=0)
    ov = ov_ref[...]
    imp = sum(_dot(part, ov) for part in _split3(p_all))
    q_pos = jnp.concatenate([lax.broadcasted_iota(jnp.int32, (t, 1), 0)] * N_KV, axis=0) + past_len
    sel = _topk_mask(_block_scores(imp, q_pos, 1), N_SEL, 1)
    sel_ref[...] = jnp.zeros(sel_ref.shape, F32)
    for c in range(sel_ref.shape[0]):
        sel_ref[c, :, 0:blk_step] = sel[:, c * blk_step:(c + 1) * blk_step]


def _sample_cmp(q, pp, hc, w2t, b2, slopes, batch, past_len):
    t = q.shape[0] // batch
    n_ch = pp.shape[1]
    n_blk = n_ch - CMP_PARTS + 1
    n_sel_blk = -(-(past_len + t) // SEL_BLOCK)
    blk_step = SEL_PAGES_PER_STEP * PAGE_SIZE // SEL_BLOCK
    n_chunk = -(-n_sel_blk // blk_step)
    ov = _overlap_matrix(n_blk, n_ch, -(-n_chunk * blk_step // LANES) * LANES, False)
    return pl.pallas_call(
        functools.partial(_sample_cmp_kernel, n_blk=n_blk, past_len=past_len, blk_step=blk_step),
        grid=(batch,),
        in_specs=[pl.BlockSpec(memory_space=pltpu.SMEM),
                  pl.BlockSpec((t, D_MODEL), lambda b: (b, 0)),
                  pl.BlockSpec((None, n_ch, N_KV * 4 * CMP_HIDDEN), lambda b: (b, 0, 0)),
                  _full(hc.shape), _full(w2t.shape), _full(b2.shape), _full(ov.shape)],
        out_specs=[pl.BlockSpec((t, D_MODEL), lambda b: (b, 0)),
                   pl.BlockSpec((None, n_chunk, N_KV * t, LANES), lambda b: (b, 0, 0, 0))],
        out_shape=[jax.ShapeDtypeStruct((batch * t, D_MODEL), F32),
                   jax.ShapeDtypeStruct((batch, n_chunk, N_KV * t, LANES), F32)],
        compiler_params=_params("parallel"),
        name="sample_cmp",
    )(slopes, q, pp, hc, w2t, b2, ov)


def _sample_attn_kernel(*refs, past_len):
    n_pg = SEL_PAGES_PER_STEP
    pg_refs = refs[1:1 + n_pg]
    (slopes_ref, q_ref, selc_ref, sell_ref, ex_ref, kvs_ref, kvw_ref, win_ref, oc_ref, gb_ref,
     o_ref, nwin_ref, m_ref, l_ref, acc_ref) = refs[1 + n_pg:]
    c = pl.program_id(1)
    n_step = pl.num_programs(1)
    t = q_ref.shape[0]
    rows = GROUP * t
    step_keys = n_pg * PAGE_SIZE
    row_t = _row_t(t)

    def update(s, valid, pv, m, l, acc):
        s = jnp.where(valid, s, NEG)
        m_new = jnp.maximum(m, jnp.max(s, axis=-1, keepdims=True))
        p = jnp.exp2(s - m_new)
        alpha = jnp.exp2(m - m_new)
        return m_new, alpha * l + jnp.sum(p, axis=-1, keepdims=True), alpha * acc + pv(p.astype(BF16))

    def finish(l, acc):
        return acc * (1.0 / jnp.maximum(l, 1e-30))

    @pl.when(c == 0)
    def _():
        m_ref[...] = jnp.full(m_ref.shape, NEG, F32)
        l_ref[...] = jnp.zeros(l_ref.shape, F32)
        acc_ref[...] = jnp.zeros(acc_ref.shape, F32)

    q = q_ref[...]
    k_rel = lax.broadcasted_iota(jnp.int32, (1, step_keys), 1) + (c * step_keys - past_len)
    k_rel_f = k_rel.astype(F32)
    expand = ex_ref[...]
    for g in range(N_KV):
        k_t = jnp.concatenate([pr[g, 0] for pr in pg_refs], axis=1).astype(BF16)
        v_t = jnp.concatenate([pr[g, 1] for pr in pg_refs], axis=1).astype(BF16)
        sel = jnp.concatenate([selc_ref[g * t:(g + 1) * t, :]] * GROUP, axis=0).astype(BF16)
        valid = (_dot(sel, expand) > 0.5) & (k_rel <= row_t)
        s = _dot(_stack_heads(q, g), k_t) + _row_slopes(slopes_ref, g, t) * k_rel_f
        m, l, acc = update(s, valid, lambda p: _dot_nt(p, v_t), m_ref[g], l_ref[g], acc_ref[g])
        m_ref[g] = m
        l_ref[g] = l
        acc_ref[g] = acc

    @pl.when(c == n_step - 1)
    def _():
        wb = win_ref.shape[-1]
        pad = jnp.zeros((LANES - t, KV_ROW), F32)
        new_s = jnp.concatenate([kvs_ref[...], pad], axis=0)
        new_w = jnp.concatenate([kvw_ref[...], pad], axis=0)
        i_new = lax.broadcasted_iota(jnp.int32, (1, LANES), 1)
        i_old = lax.broadcasted_iota(jnp.int32, (1, wb), 1) - wb
        gb = gb_ref[...]

        def new_rows_update(new, lo, valid, slope, qs, m, l, acc):
            s = _dot_nt(qs, new[:, lo:lo + HEAD_DIM].astype(BF16)) + slope * i_new.astype(F32)
            v = new[:, lo + HEAD_DIM:lo + GKV].astype(BF16)
            return update(s, valid, lambda p: _dot(p, v), m, l, acc)

        for g in range(N_KV):
            lo = g * GKV
            qs = _stack_heads(q, g)
            slope = _row_slopes(slopes_ref, g, t)
            sel_new = jnp.concatenate([sell_ref[g * t:(g + 1) * t, :]] * GROUP, axis=0)[:, 0:1]
            valid = (sel_new > 0.5) & (i_new <= row_t) & (i_new < t)
            _, l, acc = new_rows_update(new_s, lo, valid, slope, qs, m_ref[g], l_ref[g], acc_ref[g])
            o_s = finish(l, acc)
            dist = row_t - i_old
            valid = (dist >= 0) & (dist < WINDOW) & (i_old + past_len >= 0)
            s = _dot(qs, win_ref[g, 0].astype(BF16)) + slope * i_old.astype(F32)
            v_t = win_ref[g, 1].astype(BF16)
            m, l, acc = update(s, valid, lambda p: _dot_nt(p, v_t), jnp.full((rows, 1), NEG, F32),
                               jnp.zeros((rows, 1), F32), jnp.zeros((rows, HEAD_DIM), F32))
            dist = row_t - i_new
            valid = (dist >= 0) & (dist < WINDOW) & (i_new < t)
            _, l, acc = new_rows_update(new_w, lo, valid, slope, qs, m, l, acc)
            o_w = finish(l, acc)
            for r in range(GROUP):
                h = g * GROUP + r
                gate = [gb[:, g * LANES + br * GROUP + r:g * LANES + br * GROUP + r + 1] for br in range(3)]
                o_ref[:, h * HEAD_DIM:(h + 1) * HEAD_DIM] = (
                    gate[0] * oc_ref[:, h * HEAD_DIM:(h + 1) * HEAD_DIM]
                    + gate[1] * o_s[r * t:(r + 1) * t, :] + gate[2] * o_w[r * t:(r + 1) * t, :])
        new_t = pltpu.roll(new_w.T, LANES - t, 1)
        lane = lax.broadcasted_iota(jnp.int32, (KV_ROW, LANES), 1)
        old = pltpu.roll(win_ref[...].reshape(KV_ROW, wb), wb - t, 1)
        nwin_ref[:, 0:wb - LANES] = old[:, 0:wb - LANES]
        nwin_ref[:, wb - LANES:wb] = jnp.where(lane < LANES - t, old[:, wb - LANES:wb], new_t)


def _sample_attn(q, selc, kvs, kvw, cache_t, win_t, page_table, o_c, gb, slopes, layer, past_len):
    batch, n_pages = page_table.shape
    t = q.shape[0] // batch
    n_step = n_pages // SEL_PAGES_PER_STEP
    wb = win_t.shape[-1]
    expand = _block_expand_matrix(SEL_PAGES_PER_STEP * PAGE_SIZE)
    n_chunk = selc.shape[1]
    tok = lambda n: pl.BlockSpec((t, n), lambda b, c, pt: (b, 0))
    return pl.pallas_call(
        functools.partial(_sample_attn_kernel, past_len=past_len),
        grid_spec=pltpu.PrefetchScalarGridSpec(
            num_scalar_prefetch=1,
            grid=(batch, n_step),
            in_specs=[_page_spec(layer, k, SEL_PAGES_PER_STEP) for k in range(SEL_PAGES_PER_STEP)] + [
                pl.BlockSpec(memory_space=pltpu.SMEM),
                tok(D_MODEL),
                pl.BlockSpec((None, None, N_KV * t, LANES), lambda b, c, pt: (b, c, 0, 0)),
                pl.BlockSpec((None, None, N_KV * t, LANES), lambda b, c, pt: (b, n_chunk - 1, 0, 0)),
                pl.BlockSpec(expand.shape, lambda b, c, pt: (0, 0)),
                tok(KV_ROW), tok(KV_ROW),
                pl.BlockSpec((None, None, N_KV, 2, HEAD_DIM, wb), lambda b, c, pt: (layer, b, 0, 0, 0, 0)),
                tok(D_MODEL), tok(N_KV * LANES)],
            out_specs=[tok(D_MODEL), pl.BlockSpec((None, KV_ROW, wb), lambda b, c, pt: (b, 0, 0))],
            scratch_shapes=[pltpu.VMEM((N_KV, GROUP * t, 1), F32), pltpu.VMEM((N_KV, GROUP * t, 1), F32),
                            pltpu.VMEM((N_KV, GROUP * t, HEAD_DIM), F32)],
        ),
        out_shape=[jax.ShapeDtypeStruct((batch * t, D_MODEL), F32),
                   jax.ShapeDtypeStruct((batch, KV_ROW, wb), F32)],
        compiler_params=_params("parallel", "arbitrary"),
        name="sample_attn",
    )(page_table, *([cache_t] * SEL_PAGES_PER_STEP), slopes, q, selc, selc, expand, kvs, kvw, win_t, o_c, gb)


HIST = 32


def _conv_kernel(z_ref, prev_ref, hist_ref, w_ref, b_ref, y_ref, st_ref, zc_ref, zs_ref):
    i = pl.program_id(1)
    tm = z_ref.shape[0]

    @pl.when(i == 0)
    def _():
        zc_ref[0:HIST, :] = hist_ref[...]

    @pl.when(i > 0)
    def _():
        zc_ref[0:HIST, :] = prev_ref[...]

    zc_ref[HIST:HIST + tm, :] = z_ref[...]
    span = HIST + tm - SUBLANES
    for rho in range(1, SUBLANES):
        zs_ref[rho, 0:span, :] = zc_ref[rho:rho + span, :]
    off = HIST - (CONV_W - 1)
    for cb in range(C_CONV // LANES):
        cols = slice(cb * LANES, (cb + 1) * LANES)
        acc = jnp.zeros((tm, LANES), F32) + b_ref[:, cols]
        for k in range(CONV_W):
            rho = (off + k) % SUBLANES
            base = off + k - rho
            tap = zc_ref[base:base + tm, cols] if rho == 0 else zs_ref[rho, base:base + tm, cols]
            acc = acc + tap * w_ref[k:k + 1, cols]
        y_ref[:, cols] = acc

    @pl.when(i == pl.num_programs(1) - 1)
    def _():
        st_ref[...] = zc_ref[HIST + tm - (CONV_W - 1):HIST + tm, :]


def _conv(z, hist, w, b, batch, tm):
    seq = z.shape[0] // batch
    nt = seq // tm
    per = tm // HIST if tm >= HIST else 1
    z3 = z.reshape(batch, seq, C_CONV)
    if tm >= HIST:
        prev_spec = pl.BlockSpec((None, HIST, C_CONV), lambda b_, i: (b_, jnp.maximum(i * per - 1, 0), 0))
        prev = z3
    else:
        prev_spec = pl.BlockSpec((None, HIST, C_CONV), lambda b_, i: (b_, 0, 0))
        prev = hist
    y, st = pl.pallas_call(
        _conv_kernel,
        grid=(batch, nt),
        in_specs=[pl.BlockSpec((None, tm, C_CONV), lambda b_, i: (b_, i, 0)),
                  prev_spec,
                  pl.BlockSpec((None, HIST, C_CONV), lambda b_, i: (b_, 0, 0)),
                  _full(w.shape), _full(b.shape)],
        out_specs=[pl.BlockSpec((None, tm, C_CONV), lambda b_, i: (b_, i, 0)),
                   pl.BlockSpec((None, CONV_W - 1, C_CONV), lambda b_, i: (b_, 0, 0))],
        out_shape=[jax.ShapeDtypeStruct((batch, seq, C_CONV), F32),
                   jax.ShapeDtypeStruct((batch, CONV_W - 1, C_CONV), F32)],
        scratch_shapes=[pltpu.VMEM((HIST + tm, C_CONV), F32), pltpu.VMEM((SUBLANES, HIST + tm, C_CONV), F32)],
        compiler_params=_params("parallel", "arbitrary"),
        name="conv",
    )(z3, prev, hist, w, b)
    return y.reshape(batch * seq, C_CONV), st


def _post_math(y, x_ref, oa_ref, gm_ref, ple_ref, wpw_ref, wout_ref, wg_ref, wu_ref, wd_ref, wpg_ref, wpp_ref,
               cg_ref, cb_ref, bpw_ref, g1_ref, b1_ref, g2_ref, b2_ref, alpha):
    yn = _layernorm(y, cg_ref[...], cb_ref[...])
    o_b = _dot((yn * _sigmoid(yn)).astype(BF16), wpw_ref[...]) + bpw_ref[...]
    mixed = gm_ref[:, 0:D_MODEL] * oa_ref[...] + gm_ref[:, D_MODEL:2 * D_MODEL] * o_b
    x1 = _layernorm(alpha * x_ref[...] + _dot(mixed.astype(BF16), wout_ref[...]), g1_ref[...], b1_ref[...])
    x1b = x1.astype(BF16)
    hg = _dot(x1b, wg_ref[...])
    hu = _dot(x1b, wu_ref[...])
    f = _dot((hg * _sigmoid(hg) * hu).astype(BF16), wd_ref[...])
    x2 = _layernorm(alpha * x1 + f, g2_ref[...], b2_ref[...])
    gate = _sigmoid(_dot(x2.astype(BF16), wpg_ref[...]))
    return x2 + gate * _dot(ple_ref[...].astype(BF16), wpp_ref[...])


def _post_kernel(x_ref, y_ref, oa_ref, gm_ref, ple_ref, *refs, alpha):
    o_ref = refs[-1]
    o_ref[...] = _post_math(y_ref[...], x_ref, oa_ref, gm_ref, ple_ref, *refs[:-1], alpha)


def _post(x, y, o_a, gm, ple, w, alpha, tm):
    t = x.shape[0]
    row = lambda n: pl.BlockSpec((tm, n), lambda i: (i, 0))
    mats = [w["w_pw2"], w["w_out"], w["w_ffn_gate"], w["w_ffn_up"], w["w_ffn_down"], w["w_ple_gate"], w["w_ple_in"]]
    vecs = [w["conv_ln_g"], w["conv_ln_b"], w["b_pw2"], w["ln1_g"], w["ln1_b"], w["ln2_g"], w["ln2_b"]]
    return pl.pallas_call(
        functools.partial(_post_kernel, alpha=alpha),
        grid=(t // tm,),
        in_specs=[row(D_MODEL), row(C_CONV), row(D_MODEL), row(2 * D_MODEL), row(ple.shape[1])]
                 + [_resident(m.shape) for m in mats] + [_resident(v.shape) for v in vecs],
        out_specs=row(D_MODEL),
        out_shape=jax.ShapeDtypeStruct((t, D_MODEL), F32),
        compiler_params=_params("parallel"),
        name="post",
    )(x, y, o_a, gm, ple, *mats, *vecs)


def _prep_layer(l, w_in, b_in, cmp_w1, cmp_w2, cmp_b2, named):
    offs = np.cumsum((D_MODEL, 3 * KV_ROW, 3 * N_HEADS, 2 * C_CONV, 2 * D_MODEL))
    wl, bl = w_in[l], b_in[l]
    qs = ATTN_SCALE * LOG2E
    new_cols = np.array([g * LANES + br * GROUP + r for br in range(3) for g in range(N_KV) for r in range(GROUP)])
    wgb = jnp.zeros((D_MODEL, N_KV * LANES), F32).at[:, new_cols].set(wl[:, offs[1]:offs[2]])
    bgb = jnp.zeros((N_KV * LANES,), F32).at[new_cols].set(bl[offs[1]:offs[2]])
    out = {
        "wq": (wl[:, :offs[0]] * qs).astype(BF16), "bq": (bl[:offs[0]] * qs)[None],
        "wkvT": wl[:, offs[0]:offs[1]].T.astype(BF16),
        "bkv_row": bl[offs[0]:offs[1]][None], "bkv_col": bl[offs[0]:offs[1]][:, None],
        "wgb": wgb.astype(BF16), "bgb": bgb[None],
        "wu": wl[:, offs[2]:offs[3]].astype(BF16), "bu": bl[offs[2]:offs[3]][None],
        "wgm": wl[:, offs[3]:offs[4]].astype(BF16), "bgm": bl[offs[3]:offs[4]][None],
    }
    w1 = cmp_w1[l].reshape(2, CMP_PARTS, CMP_STRIDE, HEAD_DIM, CMP_HIDDEN).transpose(2, 0, 3, 1, 4)
    zero = jnp.zeros_like(w1[:, 0])
    w1cat = jnp.stack([jnp.stack([w1[:, 0], zero], axis=2), jnp.stack([zero, w1[:, 1]], axis=2)], axis=1)
    out["w1cat"] = w1cat.reshape(CMP_STRIDE * 2 * HEAD_DIM, 4 * CMP_HIDDEN).astype(BF16)
    out["w2t"] = jnp.transpose(cmp_w2[l], (0, 2, 1)).astype(BF16)
    out["b2"] = cmp_b2[l][:, :, None]
    for name, val in named.items():
        v = val[l]
        out[name] = v.astype(BF16) if v.ndim == 2 else v[None]
    return out


def _kv_out(stacked_t):
    l, b, _, t = stacked_t.shape
    return jnp.transpose(stacked_t.reshape(l, b, N_KV, 2, HEAD_DIM, t), (0, 1, 5, 2, 3, 4))


def kernel(x_prompt, x_sample, cache_cmp, cache_sel, state_win, state_conv, page_table, p_prompt, p_sample,
           w_in, b_in, cmp_w1, cmp_pe, cmp_b1, cmp_w2, cmp_b2, conv_w, conv_b, conv_ln_g, conv_ln_b,
           w_pw2, b_pw2, w_out, ln1_g, ln1_b, w_ffn_gate, w_ffn_up, w_ffn_down, ln2_g, ln2_b,
           w_ple_in, w_ple_gate):
    depth = w_in.shape[0]
    alpha = float((2 * depth) ** 0.25)
    batch, seq = x_prompt.shape[:2]
    dbatch, dseq = x_sample.shape[:2]
    past_len = page_table.shape[1] * PAGE_SIZE
    slopes = jnp.exp2(-8.0 * jnp.arange(1, N_HEADS + 1, dtype=F32) / N_HEADS) * LOG2E
    hconst = _pe_term(cmp_pe, cmp_w1, cmp_b1)
    named = dict(w_pw2=w_pw2, w_out=w_out, w_ffn_gate=w_ffn_gate, w_ffn_up=w_ffn_up, w_ffn_down=w_ffn_down,
                 w_ple_gate=w_ple_gate, w_ple_in=w_ple_in, conv_ln_g=conv_ln_g, conv_ln_b=conv_ln_b,
                 b_pw2=b_pw2, ln1_g=ln1_g, ln1_b=ln1_b, ln2_g=ln2_g, ln2_b=ln2_b, conv_b=conv_b)
    layers = [_prep_layer(l, w_in, b_in, cmp_w1, cmp_w2, cmp_b2, named) for l in range(depth)]

    x = x_prompt.reshape(batch * seq, D_MODEL)
    zero_hist = jnp.zeros((batch, HIST, C_CONV), F32)
    out_p = [[], [], [], []]
    for l, w in enumerate(layers):
        q, kvc_t, kvs_t, kvw_t, gb, z, gm = _in_proj(x, w, batch, BF16, 256, True)
        pp = _cmp_partial_prompt(kvc_t, w["w1cat"])
        o_c, sel = _prompt_select(q, pp, gb, hconst[l], w["w2t"], w["b2"], slopes, batch)
        o_a = _prompt_attn(q, sel, kvs_t, kvw_t, gb, o_c, slopes)
        y, st = _conv(z, zero_hist, conv_w[l], w["conv_b"], batch, 256)
        x = _post(x, y, o_a, gm, p_prompt[l].reshape(batch * seq, -1), w, alpha, 256)
        out_p[0].append(kvc_t)
        out_p[1].append(kvs_t)
        out_p[2].append(kvw_t[:, :, seq - min(WINDOW, seq):])
        out_p[3].append(st)
    y_prompt = x.reshape(batch, seq, D_MODEL)

    x = x_sample.reshape(dbatch * dseq, D_MODEL)
    cmp_t = _paged_view(cache_cmp)
    sel_t = _paged_view(cache_sel)
    win_t = jnp.transpose(state_win, (0, 1, 3, 4, 5, 2))
    out_s = [[], [], [], []]
    for l, w in enumerate(layers):
        q, kvc, kvs, kvw, gb, z, gm = _in_proj(x, w, 1, F32, dbatch * dseq, False)
        pp = _cmp_partial_sample(cmp_t, page_table, w["w1cat"], l)
        o_c, selc = _sample_cmp(q, pp, hconst[l], w["w2t"], w["b2"], slopes, dbatch, past_len)
        o_a, nwin = _sample_attn(q, selc, kvs, kvw, sel_t, win_t, page_table, o_c, gb, slopes, l, past_len)
        hist = jnp.pad(state_conv[l], ((0, 0), (HIST - (CONV_W - 1), 0), (0, 0)))
        y, st = _conv(z, hist, conv_w[l], w["conv_b"], dbatch, dseq)
        x = _post(x, y, o_a, gm, p_sample[l].reshape(dbatch * dseq, -1), w, alpha, dbatch * dseq)
        out_s[0].append(kvc.reshape(dbatch, dseq, N_KV, 2, HEAD_DIM))
        out_s[1].append(kvs.reshape(dbatch, dseq, N_KV, 2, HEAD_DIM))
        out_s[2].append(nwin)
        out_s[3].append(st)
    y_sample = x.reshape(dbatch, dseq, D_MODEL)

    return (y_prompt, y_sample, _kv_out(jnp.stack(out_p[0])), _kv_out(jnp.stack(out_p[1])),
            _kv_out(jnp.stack(out_p[2])), jnp.stack(out_p[3]),
            jnp.stack(out_s[0]), jnp.stack(out_s[1]), _kv_out(jnp.stack(out_s[2])), jnp.stack(out_s[3]))
```
